```python
import jax, jax.numpy as jnp
from jax import lax
import numpy as np

D_MODEL = 1024
BATCH = 16
SEQ = 2048
DEPTH = 2

N_A = DEPTH // 2
N_B = DEPTH - N_A
N_META = 16
A_EXPAND = 128
A_HEADS = D_MODEL // A_EXPAND
A_DV = D_MODEL // A_HEADS
A_CHUNK = 64
B_HEADS = 16
B_HDIM = D_MODEL // B_HEADS
Q_BLOCK = 128
FG_BIAS_INIT = 2.0
D_FF = ((8 * D_MODEL // 3 + 63) // 64) * 64
CONV_W = 3
EPS = 1e-6

kernel_name = "yoco_hgrn2_fox_hybrid"


def rmsnorm(x, g):
    xf = x.astype(jnp.float32)
    y = xf * lax.rsqrt(jnp.mean(xf * xf, axis=-1, keepdims=True) + EPS)
    return (y * g.astype(jnp.float32)).astype(x.dtype)


def causal_dwconv(u, w):
    C = u.shape[-1]
    return lax.conv_general_dilated(
        u, w[:, None, :].astype(u.dtype), window_strides=(1,),
        padding=[(CONV_W - 1, 0)], dimension_numbers=("NWC", "WIO", "NWC"),
        feature_group_count=C)


def conv_ffn(x, w_up, conv_w, w_down):
    u = causal_dwconv(x @ w_up, conv_w)
    gate, val = jnp.split(u, 2, axis=-1)
    return (jax.nn.silu(gate) * val) @ w_down


def gla_chunks(q, k, v, logf, s0):
    b = jnp.cumsum(logf, axis=3)
    b_last = b[..., -1:, :]
    q_in = q * jnp.exp(b)
    k_in = k * jnp.exp(-b)
    k_out = k * jnp.exp(b_last - b)
    C = q.shape[3]
    causal = jnp.tril(jnp.ones((C, C), dtype=bool))
    attn = jnp.where(causal, jnp.einsum("bhnck,bhnsk->bhncs", q_in, k_in), 0.0)
    o_intra = jnp.einsum("bhncs,bhnsv->bhncv", attn, v)
    dS = jnp.einsum("bhnck,bhncv->bhnkv", k_out, v)
    decay = jnp.exp(b_last[..., 0, :])

    def step(S, inp):
        d, ds = inp
        return d[..., :, None] * S + ds, S

    S_fin, S_in = lax.scan(step, s0, (jnp.moveaxis(decay, 2, 0), jnp.moveaxis(dS, 2, 0)))
    S_in = jnp.moveaxis(S_in, 0, 2)
    o_inter = jnp.einsum("bhnck,bhnkv->bhncv", q_in, S_in)
    return o_intra + o_inter, S_fin


def hgrn2_mixer(x, w_in, lb, head_gain, w_out):
    Bsz, T, _ = x.shape
    q, f_pre, i, g = jnp.split(x @ w_in, 4, axis=-1)
    f = lb + (1.0 - lb) * jax.nn.sigmoid(f_pre.astype(jnp.float32))
    logf = jnp.log(f)
    k = 1.0 - f

    def heads(t):
        return t.astype(jnp.float32).reshape(Bsz, T, A_HEADS, -1).transpose(0, 2, 1, 3)

    qh, kh, vh, lh = heads(q), heads(k), heads(i), heads(logf)

    def meta_part(t):
        return t[:, :, :N_META][:, :, None]

    def real_part(t):
        return t[:, :, N_META:].reshape(Bsz, A_HEADS, -1, A_CHUNK, t.shape[-1])

    s0 = jnp.zeros((Bsz, A_HEADS, A_EXPAND, A_DV), jnp.float32)
    o_meta, s_meta = gla_chunks(meta_part(qh), meta_part(kh), meta_part(vh), meta_part(lh), s0)
    o_real, _ = gla_chunks(real_part(qh), real_part(kh), real_part(vh), real_part(lh), s_meta)
    o = jnp.concatenate([o_meta.reshape(Bsz, A_HEADS, N_META, A_DV),
                         o_real.reshape(Bsz, A_HEADS, T - N_META, A_DV)], axis=2)
    o = o.transpose(0, 2, 1, 3)
    o = o * lax.rsqrt(jnp.mean(o * o, axis=-1, keepdims=True) + EPS)
    o = o * head_gain.astype(jnp.float32).reshape(A_HEADS, A_DV)
    o = o.reshape(Bsz, T, D_MODEL).astype(x.dtype) * jax.nn.silu(g)
    return o @ w_out


def shared_kv(h, kv_norm, kv_w, fg_b):
    Bsz, T, _ = h.shape
    proj = rmsnorm(h, kv_norm) @ kv_w
    k = proj[..., :D_MODEL].reshape(Bsz, T, B_HEADS, B_HDIM).transpose(0, 2, 1, 3)
    v = proj[..., D_MODEL:2 * D_MODEL].reshape(Bsz, T, B_HEADS, B_HDIM).transpose(0, 2, 1, 3)
    zf = proj[..., 2 * D_MODEL:].astype(jnp.float32) + fg_b.astype(jnp.float32)
    c = jnp.cumsum(jax.nn.log_sigmoid(zf), axis=1).transpose(0, 2, 1)
    return k, v, c


def fox_mixer(x, w_q, w_out, k, v, c):
    Bsz, T, _ = x.shape
    q = (x @ w_q).reshape(Bsz, T, B_HEADS, B_HDIM).transpose(0, 2, 1, 3)
    scale = 1.0 / np.sqrt(B_HDIM).astype(np.float32)
    bounds = [(0, N_META)] + [(N_META + n * Q_BLOCK, N_META + (n + 1) * Q_BLOCK)
                              for n in range((T - N_META) // Q_BLOCK)]
    outs = []
    for s, e in bounds:
        logits = jnp.einsum("bhqd,bhkd->bhqk", q[:, :, s:e], k[:, :, :e]).astype(jnp.float32) * scale
        logits = logits + (c[:, :, s:e, None] - c[:, :, None, :e])
        mask = jnp.arange(s, e)[:, None] >= jnp.arange(e)[None, :]
        p = jax.nn.softmax(jnp.where(mask, logits, -1e30), axis=-1)
        outs.append(jnp.einsum("bhqk,bhkd->bhqd", p.astype(v.dtype), v[:, :, :e]))
    o = jnp.concatenate(outs, axis=2).transpose(0, 2, 1, 3).reshape(Bsz, T, D_MODEL)
    return o @ w_out


def setup_inputs(seed: int = 0) -> dict:
    key = jax.random.key(seed)
    ks = jax.random.split(key, 16)
    D = D_MODEL

    def nrm(k, shape, scale):
        return jax.random.normal(k, shape, jnp.float32) * scale

    return {
        "x": nrm(ks[0], (BATCH, SEQ, D), 1.0),
        "meta_tokens": nrm(ks[1], (N_META, D), 1.0),
        "norm_gains": 1.0 + nrm(ks[2], (DEPTH, 4, D), 0.05),
        "a_w_in": nrm(ks[3], (N_A, D, 4 * D), D ** -0.5),
        "a_lb_logits": nrm(ks[4], (N_A + 1, D), 0.1),
        "a_head_norm": 1.0 + nrm(ks[5], (N_A, D), 0.05),
        "a_w_out": nrm(ks[6], (N_A, D, D), D ** -0.5),
        "kv_norm": 1.0 + nrm(ks[7], (D,), 0.05),
        "kv_w": nrm(ks[8], (D, 2 * D + B_HEADS), D ** -0.5),
        "fg_b": FG_BIAS_INIT + nrm(ks[9], (B_HEADS,), 0.1),
        "b_w_q": nrm(ks[10], (N_B, D, D), D ** -0.5),
        "b_w_out": nrm(ks[11], (N_B, D, D), D ** -0.5),
        "ffn_w_up": nrm(ks[12], (DEPTH, D, 2 * D_FF), D ** -0.5),
        "ffn_conv": nrm(ks[13], (DEPTH, CONV_W, 2 * D_FF), CONV_W ** -0.5),
        "ffn_w_down": nrm(ks[14], (DEPTH, D_FF, D), D_FF ** -0.5),
    }


def reference(x, meta_tokens, norm_gains, a_w_in, a_lb_logits, a_head_norm, a_w_out,
              kv_norm, kv_w, fg_b, b_w_q, b_w_out, ffn_w_up, ffn_conv, ffn_w_down):
    Bsz = x.shape[0]
    meta = jnp.broadcast_to(meta_tokens[None].astype(x.dtype), (Bsz, N_META, D_MODEL))
    h = jnp.concatenate([meta, x], axis=1)
    lb_all = jnp.cumsum(jax.nn.softmax(a_lb_logits.astype(jnp.float32), axis=0), axis=0)
    k_sh = v_sh = c_sh = None
    for l in range(DEPTH):
        g = norm_gains[l]
        hn = rmsnorm(h, g[0])
        if l < N_A:
            mix = hgrn2_mixer(hn, a_w_in[l], lb_all[l], a_head_norm[l], a_w_out[l])
        else:
            if l == N_A:
                k_sh, v_sh, c_sh = shared_kv(h, kv_norm, kv_w, fg_b)
            j = l - N_A
            mix = fox_mixer(hn, b_w_q[j], b_w_out[j], k_sh, v_sh, c_sh)
        h = h + rmsnorm(mix, g[1])
        ff = conv_ffn(rmsnorm(h, g[2]), ffn_w_up[l], ffn_conv[l], ffn_w_down[l])
        h = h + rmsnorm(ff, g[3])
    return h[:, N_META:]
```

```python
import functools

import numpy as np
import jax
import jax.numpy as jnp
from jax import lax
from jax.experimental import pallas as pl
from jax.experimental.pallas import tpu as pltpu

F32 = jnp.float32
BF16 = jnp.bfloat16

D_MODEL = 1024
N_META = 16
A_HEADS = 8
A_DK = D_MODEL // A_HEADS
A_CHUNK = 64
B_HEADS = 16
B_HDIM = D_MODEL // B_HEADS
D_FF = 2752
EPS = 1e-6

LANES = 128
SUBLANES = 8
MXU_DIM = 256
D_FF_PAD = -(-D_FF // MXU_DIM) * MXU_DIM
ATT_PAD = LANES
VMEM_LIMIT = 56 * 1024 * 1024

SEQ_TILE = 256
Q_TILE = 256
CONV_ROWS = 64
CONV_COLS = 256
HEADS_PER_STEP = 2


def _rms(x, g):
    return x * lax.rsqrt(jnp.mean(x * x, axis=-1, keepdims=True) + EPS) * g


def _sigmoid(x):
    return 1.0 / (1.0 + jnp.exp(-x))


def _split3(x):
    hi = x.astype(BF16)
    r = x - hi.astype(F32)
    mid = r.astype(BF16)
    lo = (r - mid.astype(F32)).astype(BF16)
    return hi, mid, lo


def _tril(n):
    row = lax.broadcasted_iota(jnp.int32, (n, n), 0)
    col = lax.broadcasted_iota(jnp.int32, (n, n), 1)
    return row >= col


def _cumsum_rows(x, tril_bf16):
    hi, mid, lo = _split3(x)
    dot = functools.partial(jnp.dot, preferred_element_type=F32)
    return dot(tril_bf16, hi) + dot(tril_bf16, mid) + dot(tril_bf16, lo)


def _dot_nt(a, b):
    return lax.dot_general(a, b, (((1,), (1,)), ((), ())), preferred_element_type=F32)


def _const_spec(shape):
    nd = len(shape)
    return pl.BlockSpec(shape, lambda *_: (0,) * nd, pipeline_mode=pl.Buffered(1))


def _params():
    return pltpu.CompilerParams(dimension_semantics=("arbitrary", "arbitrary"),
                                vmem_limit_bytes=VMEM_LIMIT)


def _hgrn2_kernel(x_ref, gpre_ref, gpost_ref, lbl_ref, hg_ref, win_ref, wout_ref, s0_ref,
                  out_ref, sfin_ref, proj_s, y_s, st_s, *, ts, chunk, layer):
    t = pl.program_id(1)

    @pl.when(t == 0)
    def _():
        st_s[...] = s0_ref[...]

    lbl = lbl_ref[...]
    e = jnp.exp(lbl - jnp.max(lbl, axis=0, keepdims=True))
    lb = jnp.sum(e[:layer + 1], axis=0, keepdims=True) / jnp.sum(e, axis=0, keepdims=True)

    xn = _rms(x_ref[0], gpre_ref[...]).astype(BF16)
    proj_s[...] = jnp.dot(xn, win_ref[...], preferred_element_type=F32)

    causal = _tril(chunk)
    tril_bf16 = causal.astype(BF16)
    hgain = hg_ref[...]

    def chunk_body(c, carry):
        rows = pl.ds(pl.multiple_of(c * chunk, chunk), chunk)
        for h in range(A_HEADS):
            lo = h * A_DK
            q = proj_s[rows, pl.ds(lo, A_DK)]
            fp = proj_s[rows, pl.ds(D_MODEL + lo, A_DK)]
            iv = proj_s[rows, pl.ds(2 * D_MODEL + lo, A_DK)]
            gg = proj_s[rows, pl.ds(3 * D_MODEL + lo, A_DK)]
            lbh = lb[:, lo:lo + A_DK]
            f = lbh + (1.0 - lbh) * _sigmoid(fp)
            k = 1.0 - f
            b = _cumsum_rows(jnp.log(f), tril_bf16)
            b_last = b[chunk - 1:chunk, :]
            q_in = (q * jnp.exp(b)).astype(BF16)
            k_in = (k * jnp.exp(-b)).astype(BF16)
            k_out = (k * jnp.exp(b_last - b)).astype(BF16)
            v = iv.astype(BF16)
            v_t = iv.T.astype(BF16)
            attn = jnp.where(causal, _dot_nt(q_in, k_in), 0.0).astype(BF16)
            st = st_s[h]
            o = (jnp.dot(attn, v, preferred_element_type=F32)
                 + _dot_nt(q_in, st.astype(BF16)))
            st_s[h] = st * jnp.exp(b_last) + jnp.dot(v_t, k_out, preferred_element_type=F32)
            o = o * lax.rsqrt(jnp.mean(o * o, axis=-1, keepdims=True) + EPS)
            y = o * hgain[:, lo:lo + A_DK] * (gg * _sigmoid(gg))
            y_s[rows, pl.ds(lo, A_DK)] = y.astype(BF16)
        return carry

    lax.fori_loop(0, ts // chunk, chunk_body, 0)

    mix = jnp.dot(y_s[...], wout_ref[...], preferred_element_type=F32)
    out_ref[0] = x_ref[0] + _rms(mix, gpost_ref[...])

    @pl.when(t == pl.num_programs(1) - 1)
    def _():
        sfin_ref[0] = st_s[...]


def _hgrn2_layer(h, g_pre, g_post, lb_logits, head_gain, w_in, w_out, s0, *, ts, chunk, layer):
    bsz, t_len, d = h.shape
    assert t_len % ts == 0 and ts % chunk == 0
    row = lambda v: v.reshape(1, d)
    st_shape = (A_HEADS, A_DK, A_DK)
    out, s_fin = pl.pallas_call(
        functools.partial(_hgrn2_kernel, ts=ts, chunk=chunk, layer=layer),
        grid=(bsz, t_len // ts),
        in_specs=[
            pl.BlockSpec((1, ts, d), lambda b, t: (b, t, 0)),
            _const_spec((1, d)), _const_spec((1, d)), _const_spec(lb_logits.shape), _const_spec((1, d)),
            _const_spec(w_in.shape), _const_spec(w_out.shape), _const_spec(st_shape),
        ],
        out_specs=[
            pl.BlockSpec((1, ts, d), lambda b, t: (b, t, 0)),
            pl.BlockSpec((1,) + st_shape, lambda b, t: (b, 0, 0, 0)),
        ],
        out_shape=[jax.ShapeDtypeStruct(h.shape, F32),
                   jax.ShapeDtypeStruct((bsz,) + st_shape, F32)],
        scratch_shapes=[pltpu.VMEM((ts, 4 * d), F32), pltpu.VMEM((ts, d), BF16),
                        pltpu.VMEM(st_shape, F32)],
        compiler_params=_params(),
        name="hgrn2_mixer",
    )(h, row(g_pre), row(g_post), lb_logits, row(head_gain), w_in, w_out, s0)
    return out, s_fin


def _ffn_kernel(*refs, ts, with_attn):
    if with_attn:
        (h_ref, o_ref, wo_ref, gmix_ref, g2_ref, g3_ref, wup_ref, conv_ref, wdown_ref, tail0_ref,
         out_ref, tail_ref, u_s, act_s) = refs
    else:
        (h_ref, g2_ref, g3_ref, wup_ref, conv_ref, wdown_ref, tail0_ref,
         out_ref, tail_ref, u_s, act_s) = refs
    t = pl.program_id(1)
    halo = SUBLANES

    @pl.when(t == 0)
    def _():
        u_s[0:halo, :] = tail0_ref[...]

    h = h_ref[0]
    if with_attn:
        mix = jnp.dot(o_ref[0], wo_ref[...], preferred_element_type=F32)
        h = h + _rms(mix, gmix_ref[...])
    xn = _rms(h, g2_ref[...]).astype(BF16)
    u_s[halo:halo + ts, :] = jnp.dot(xn, wup_ref[...], preferred_element_type=F32)

    rb = min(CONV_ROWS, ts)

    def row_body(r, carry):
        r0 = pl.multiple_of(r * rb, rb)
        for j in range(D_FF_PAD // CONV_COLS):
            def conv(c0):
                blk = u_s[pl.ds(r0, rb + halo), pl.ds(c0, CONV_COLS)]
                w = conv_ref[:, pl.ds(c0, CONV_COLS)]
                return (w[0:1] * blk[halo - 2:halo - 2 + rb] + w[1:2] * blk[halo - 1:halo - 1 + rb]
                        + w[2:3] * blk[halo:halo + rb])
            cg = conv(j * CONV_COLS)
            cv = conv(D_FF_PAD + j * CONV_COLS)
            act_s[pl.ds(r0, rb), pl.ds(j * CONV_COLS, CONV_COLS)] = (cg * _sigmoid(cg) * cv).astype(BF16)
        return carry

    lax.fori_loop(0, ts // rb, row_body, 0)
    u_s[0:halo, :] = u_s[ts:ts + halo, :]

    ff = jnp.dot(act_s[...], wdown_ref[...], preferred_element_type=F32)
    out_ref[0] = h + _rms(ff, g3_ref[...])

    @pl.when(t == pl.num_programs(1) - 1)
    def _():
        tail_ref[0] = u_s[0:halo, :]


def _ffn_layer(h, g2, g3, w_up, conv_w, w_down, tail0, *, ts, attn=None):
    bsz, t_len, d = h.shape
    assert t_len % ts == 0
    row = lambda v: v.reshape(1, d)
    tile = pl.BlockSpec((1, ts, d), lambda b, t: (b, t, 0))
    args, specs = [h], [tile]
    if attn is not None:
        o, w_o, g_mix = attn
        args += [o, w_o, row(g_mix)]
        specs += [tile, _const_spec(w_o.shape), _const_spec((1, d))]
    args += [row(g2), row(g3), w_up, conv_w, w_down, tail0]
    specs += [_const_spec((1, d)), _const_spec((1, d)), _const_spec(w_up.shape), _const_spec(conv_w.shape),
              _const_spec(w_down.shape), _const_spec(tail0.shape)]
    out, tail = pl.pallas_call(
        functools.partial(_ffn_kernel, ts=ts, with_attn=attn is not None),
        grid=(bsz, t_len // ts),
        in_specs=specs,
        out_specs=[tile, pl.BlockSpec((1, SUBLANES, 2 * D_FF_PAD), lambda b, t: (b, 0, 0))],
        out_shape=[jax.ShapeDtypeStruct(h.shape, F32),
                   jax.ShapeDtypeStruct((bsz, SUBLANES, 2 * D_FF_PAD), F32)],
        scratch_shapes=[pltpu.VMEM((ts + SUBLANES, 2 * D_FF_PAD), F32), pltpu.VMEM((ts, D_FF_PAD), BF16)],
        compiler_params=_params(),
        name="conv_ffn",
    )(*args)
    return out, tail


def _fox_proj_kernel(h_ref, gq_ref, gkv_ref, wq_ref, wk_ref, wv_ref, wfh_ref, wfl_ref, fgb_ref, sel_ref,
                     c0_ref, q_out, k_out, v_out, cend_ref, q_s, k_s, v_s, aux_s, carry_s, *, ts):
    t = pl.program_id(1)

    @pl.when(t == 0)
    def _():
        carry_s[...] = c0_ref[...]

    h = h_ref[0]
    y = h * lax.rsqrt(jnp.mean(h * h, axis=-1, keepdims=True) + EPS)
    xq = (y * gq_ref[...]).astype(BF16)
    xkv = y * gkv_ref[...]
    xkv_hi = xkv.astype(BF16)
    xkv_lo = (xkv - xkv_hi.astype(F32)).astype(BF16)
    dot = functools.partial(jnp.dot, preferred_element_type=F32)
    q_s[...] = dot(xq, wq_ref[...]) * (1.0 / np.sqrt(B_HDIM).astype(np.float32))
    k_s[...] = dot(xkv_hi, wk_ref[...])
    v_s[...] = dot(xkv_hi, wv_ref[...])
    zf = (dot(xkv_hi, wfh_ref[...]) + dot(xkv_lo, wfh_ref[...]) + dot(xkv_hi, wfl_ref[...])
          + fgb_ref[...])
    ls = jnp.minimum(zf, 0.0) - jnp.log(1.0 + jnp.exp(-jnp.abs(zf)))
    c = carry_s[...] + _cumsum_rows(ls, _tril(ts).astype(BF16))
    carry_s[...] = c[ts - 1:ts, :]

    c_hi = c.astype(BF16).astype(F32)
    r = c - c_hi
    c_mid = r.astype(BF16).astype(F32)
    c_lo = (r - c_mid).astype(BF16).astype(F32)
    lane = lax.broadcasted_iota(jnp.int32, (ts, LANES), 1)
    parts = jnp.where(lane < B_HEADS, c_hi,
                      jnp.where(lane < 2 * B_HEADS, pltpu.roll(c_mid, B_HEADS, 1),
                                jnp.where(lane < 3 * B_HEADS, pltpu.roll(c_lo, 2 * B_HEADS, 1), 0.0)))
    aux_s[...] = dot(parts.astype(BF16), sel_ref[...])

    for hd in range(B_HEADS):
        slab = pl.ds((hd // 2) * LANES, LANES)
        def head(ref):
            x = ref[:, slab]
            return pltpu.roll(x, B_HDIM, 1) if hd % 2 else x
        a = aux_s[:, pl.ds(hd * ATT_PAD, ATT_PAD)]
        data = lane < B_HDIM
        plus = lane < B_HDIM + 3
        minus = lane < B_HDIM + 6
        q_out[0, hd] = jnp.where(data, head(q_s), jnp.where(plus, a, jnp.where(minus, 1.0, 0.0))).astype(BF16)
        k_out[0, hd] = jnp.where(data, head(k_s), jnp.where(plus, 1.0, jnp.where(minus, a, 0.0))).astype(BF16)
        v_out[0, hd] = jnp.where(data, head(v_s), jnp.where(lane == B_HDIM, 1.0, 0.0)).astype(BF16)

    @pl.when(t == pl.num_programs(1) - 1)
    def _():
        cend_ref[0] = carry_s[...]


def _fox_sel_matrix():
    sel = np.zeros((LANES, B_HEADS * ATT_PAD), np.float32)
    for term in range(3):
        for hd in range(B_HEADS):
            sel[term * B_HEADS + hd, hd * ATT_PAD + B_HDIM + term] = 1.0
            sel[term * B_HEADS + hd, hd * ATT_PAD + B_HDIM + 3 + term] = -1.0
    return jnp.asarray(sel, BF16)


def _fox_proj(h, g_q, g_kv, w_q, w_k, w_v, w_f_hi, w_f_lo, fg_b, c0, *, ts):
    bsz, t_len, d = h.shape
    assert t_len % ts == 0
    row = lambda v: v.reshape(1, d)
    sel = _fox_sel_matrix()
    head_shape = (bsz, B_HEADS, t_len, ATT_PAD)
    head_spec = pl.BlockSpec((1, B_HEADS, ts, ATT_PAD), lambda b, t: (b, 0, t, 0))
    return pl.pallas_call(
        functools.partial(_fox_proj_kernel, ts=ts),
        grid=(bsz, t_len // ts),
        in_specs=[
            pl.BlockSpec((1, ts, d), lambda b, t: (b, t, 0)),
            _const_spec((1, d)), _const_spec((1, d)),
            _const_spec(w_q.shape), _const_spec(w_k.shape), _const_spec(w_v.shape),
            _const_spec(w_f_hi.shape), _const_spec(w_f_lo.shape), _const_spec(fg_b.shape),
            _const_spec(sel.shape), _const_spec(c0.shape),
        ],
        out_specs=[head_spec, head_spec, head_spec,
                   pl.BlockSpec((1, 1, LANES), lambda b, t: (b, 0, 0))],
        out_shape=[jax.ShapeDtypeStruct(head_shape, BF16)] * 3 + [jax.ShapeDtypeStruct((bsz, 1, LANES), F32)],
        scratch_shapes=[pltpu.VMEM((ts, d), F32)] * 3
                       + [pltpu.VMEM((ts, B_HEADS * ATT_PAD), F32), pltpu.VMEM((1, LANES), F32)],
        compiler_params=_params(),
        name="fox_proj",
    )(h, row(g_q), row(g_kv), w_q, w_k, w_v, w_f_hi, w_f_lo, fg_b, sel, c0)


def _fox_attn_kernel(*refs, tq, hp, has_meta):
    if has_meta:
        q_ref, k_ref, v_ref, km_ref, vm_ref, o_ref = refs
    else:
        q_ref, k_ref, v_ref, o_ref = refs
    t_len = q_ref.shape[2]
    causal = _tril(tq)

    def update(s, v, m, acc):
        m_new = jnp.maximum(m, jnp.max(s, axis=-1, keepdims=True))
        p = jnp.exp(s - m_new)
        acc = jnp.exp(m - m_new) * acc + jnp.dot(p.astype(BF16), v, preferred_element_type=F32)
        return m_new, acc

    def q_body(qi, carry):
        q0 = pl.multiple_of(qi * tq, tq)
        outs = []
        for hh in range(hp):
            q = q_ref[0, hh, pl.ds(q0, tq), :]
            s = jnp.where(causal, _dot_nt(q, k_ref[0, hh, pl.ds(q0, tq), :]), -1e30)
            m = jnp.max(s, axis=-1, keepdims=True)
            acc = jnp.dot(jnp.exp(s - m).astype(BF16), v_ref[0, hh, pl.ds(q0, tq), :],
                          preferred_element_type=F32)
            if has_meta:
                m, acc = update(_dot_nt(q, km_ref[0, hh]), vm_ref[0, hh], m, acc)

            def kv_body(j, mc):
                k0 = pl.multiple_of(j * tq, tq)
                return update(_dot_nt(q, k_ref[0, hh, pl.ds(k0, tq), :]),
                              v_ref[0, hh, pl.ds(k0, tq), :], *mc)

            m, acc = lax.fori_loop(0, qi, kv_body, (m, acc))
            outs.append(acc[:, :B_HDIM] / acc[:, B_HDIM:B_HDIM + 1])
        o_ref[0, pl.ds(q0, tq), :] = jnp.concatenate(outs, axis=-1).astype(BF16)
        return carry

    lax.fori_loop(0, t_len // tq, q_body, 0)


def _fox_attn(q, k, v, meta_kv, *, tq, hp=HEADS_PER_STEP):
    bsz, heads, t_len, pad = q.shape
    assert t_len % tq == 0 and heads % hp == 0
    head_spec = pl.BlockSpec((1, hp, t_len, pad), lambda b, g: (b, g, 0, 0))
    args, specs = [q, k, v], [head_spec] * 3
    if meta_kv is not None:
        n_meta = meta_kv[0].shape[2]
        args += list(meta_kv)
        specs += [pl.BlockSpec((1, hp, n_meta, pad), lambda b, g: (0, g, 0, 0))] * 2
    return pl.pallas_call(
        functools.partial(_fox_attn_kernel, tq=tq, hp=hp, has_meta=meta_kv is not None),
        grid=(bsz, heads // hp),
        in_specs=specs,
        out_specs=pl.BlockSpec((1, t_len, hp * B_HDIM), lambda b, g: (b, 0, g)),
        out_shape=jax.ShapeDtypeStruct((bsz, t_len, heads * B_HDIM), BF16),
        compiler_params=_params(),
        name="fox_attention",
    )(*args)


def _pad_cols(w, n):
    return jnp.pad(w, ((0, 0), (0, n - w.shape[1])))


def _prep_ffn(w_up, conv_w, w_down):
    up = jnp.concatenate([_pad_cols(w_up[:, :D_FF], D_FF_PAD), _pad_cols(w_up[:, D_FF:], D_FF_PAD)], axis=1)
    cw = jnp.concatenate([_pad_cols(conv_w[:, :D_FF], D_FF_PAD), _pad_cols(conv_w[:, D_FF:], D_FF_PAD)], axis=1)
    down = jnp.pad(w_down, ((0, D_FF_PAD - D_FF), (0, 0)))
    return up.astype(BF16), cw.astype(F32), down.astype(BF16)


def _trunk(h, params, carry_in, *, ts, chunk, tq):
    s0, tail_a, tail_b, c0, meta_kv = carry_in
    p = params
    h1, s_fin = _hgrn2_layer(h, p["g"][0, 0], p["g"][0, 1], p["lb_logits"], p["head_gain"], p["w_in"],
                             p["a_w_out"], s0, ts=ts, chunk=chunk, layer=0)
    h2, tail_a_out = _ffn_layer(h1, p["g"][0, 2], p["g"][0, 3], *p["ffn"][0], tail_a, ts=ts)
    q, k, v, c_end = _fox_proj(h2, p["g"][1, 0], p["kv_norm"], p["w_q"], p["w_k"], p["w_v"],
                               p["w_f_hi"], p["w_f_lo"], p["fg_b"], c0, ts=ts)
    o = _fox_attn(q, k, v, meta_kv, tq=tq)
    h4, tail_b_out = _ffn_layer(h2, p["g"][1, 2], p["g"][1, 3], *p["ffn"][1], tail_b, ts=ts,
                                attn=(o, p["b_w_out"], p["g"][1, 1]))
    return h4, (s_fin[0], tail_a_out[0], tail_b_out[0], c_end[0], (k[:1], v[:1]))


def kernel(x, meta_tokens, norm_gains, a_w_in, a_lb_logits, a_head_norm, a_w_out, kv_norm, kv_w, fg_b,
           b_w_q, b_w_out, ffn_w_up, ffn_conv, ffn_w_down):
    d = D_MODEL
    assert x.shape[2] == d and x.shape[1] % SEQ_TILE == 0 and meta_tokens.shape == (N_META, d)
    assert norm_gains.shape[0] == 2 and a_w_in.shape[0] == 1 and b_w_q.shape[0] == 1

    w_f = jnp.pad(kv_w[:, 2 * d:], ((0, 0), (0, LANES - B_HEADS))).astype(F32)
    w_f_hi = w_f.astype(BF16)
    params = {
        "g": norm_gains.astype(F32),
        "lb_logits": a_lb_logits.astype(F32),
        "head_gain": a_head_norm[0].astype(F32),
        "w_in": a_w_in[0].astype(BF16),
        "a_w_out": a_w_out[0].astype(BF16),
        "kv_norm": kv_norm.astype(F32),
        "w_q": b_w_q[0].astype(BF16),
        "w_k": kv_w[:, :d].astype(BF16),
        "w_v": kv_w[:, d:2 * d].astype(BF16),
        "w_f_hi": w_f_hi,
        "w_f_lo": (w_f - w_f_hi.astype(F32)).astype(BF16),
        "fg_b": jnp.pad(fg_b.astype(F32), (0, LANES - B_HEADS)).reshape(1, LANES),
        "b_w_out": b_w_out[0].astype(BF16),
        "ffn": [_prep_ffn(ffn_w_up[l], ffn_conv[l], ffn_w_down[l]) for l in range(2)],
    }

    zero_carry = (jnp.zeros((A_HEADS, A_DK, A_DK), F32),
                  jnp.zeros((SUBLANES, 2 * D_FF_PAD), F32), jnp.zeros((SUBLANES, 2 * D_FF_PAD), F32),
                  jnp.zeros((1, LANES), F32), None)
    _, meta_carry = _trunk(meta_tokens[None].astype(F32), params, zero_carry,
                           ts=N_META, chunk=N_META, tq=N_META)
    out, _ = _trunk(x, params, meta_carry, ts=SEQ_TILE, chunk=A_CHUNK, tq=Q_TILE)
    return out
```

```python
import functools

import numpy as np
import jax
import jax.numpy as jnp
from jax import lax
from jax.experimental import pallas as pl
from jax.experimental.pallas import tpu as pltpu

F32 = jnp.float32
BF16 = jnp.bfloat16

D_MODEL = 1024
N_META = 16
A_HEADS = 8
A_DK = D_MODEL // A_HEADS
A_CHUNK = 64
B_HEADS = 16
B_HDIM = D_MODEL // B_HEADS
D_FF = 2752
EPS = 1e-6

LANES = 128
SUBLANES = 8
MXU_DIM = 256
D_FF_PAD = -(-D_FF // MXU_DIM) * MXU_DIM
ATT_PAD = LANES
VMEM_LIMIT = 56 * 1024 * 1024

SEQ_TILE = 256
Q_TILE = 256
CONV_ROWS = 64
CONV_COLS = 256
HEADS_PER_STEP = 2
ATT_STRIP = 16
MASKED = -1e30


def _rms(x, g):
    return x * lax.rsqrt(jnp.mean(x * x, axis=-1, keepdims=True) + EPS) * g


def _sigmoid(x):
    return 1.0 / (1.0 + jnp.exp(-x))


def _split3(x):
    hi = x.astype(BF16)
    r = x - hi.astype(F32)
    mid = r.astype(BF16)
    lo = (r - mid.astype(F32)).astype(BF16)
    return hi, mid, lo


def _tril(n):
    row = lax.broadcasted_iota(jnp.int32, (n, n), 0)
    col = lax.broadcasted_iota(jnp.int32, (n, n), 1)
    return row >= col


def _cumsum_rows(x, tril_bf16):
    hi, mid, lo = _split3(x)
    dot = functools.partial(jnp.dot, preferred_element_type=F32)
    return dot(tril_bf16, hi) + dot(tril_bf16, mid) + dot(tril_bf16, lo)


def _dot_nt(a, b):
    return lax.dot_general(a, b, (((1,), (1,)), ((), ())), preferred_element_type=F32)


def _const_spec(shape):
    nd = len(shape)
    return pl.BlockSpec(shape, lambda *_: (0,) * nd, pipeline_mode=pl.Buffered(1))


def _params():
    return pltpu.CompilerParams(dimension_semantics=("arbitrary", "arbitrary"),
                                vmem_limit_bytes=VMEM_LIMIT)


def _hgrn2_kernel(x_ref, gpre_ref, gpost_ref, lbl_ref, hg_ref, win_ref, wout_ref, s0_ref,
                  out_ref, sfin_ref, proj_s, y_s, st_s, *, ts, chunk, layer):
    t = pl.program_id(1)

    @pl.when(t == 0)
    def _():
        st_s[...] = s0_ref[...]

    lbl = lbl_ref[...]
    e = jnp.exp(lbl - jnp.max(lbl, axis=0, keepdims=True))
    lb = jnp.sum(e[:layer + 1], axis=0, keepdims=True) / jnp.sum(e, axis=0, keepdims=True)

    xn = _rms(x_ref[0], gpre_ref[...]).astype(BF16)
    proj_s[...] = jnp.dot(xn, win_ref[...], preferred_element_type=F32)

    causal = _tril(chunk)
    tril_bf16 = causal.astype(BF16)
    hgain = hg_ref[...]

    def chunk_body(c, carry):
        rows = pl.ds(pl.multiple_of(c * chunk, chunk), chunk)
        for h in range(A_HEADS):
            lo = h * A_DK
            q = proj_s[rows, pl.ds(lo, A_DK)]
            fp = proj_s[rows, pl.ds(D_MODEL + lo, A_DK)]
            iv = proj_s[rows, pl.ds(2 * D_MODEL + lo, A_DK)]
            gg = proj_s[rows, pl.ds(3 * D_MODEL + lo, A_DK)]
            lbh = lb[:, lo:lo + A_DK]
            f = lbh + (1.0 - lbh) * _sigmoid(fp)
            k = 1.0 - f
            b = _cumsum_rows(jnp.log(f), tril_bf16)
            b_last = b[chunk - 1:chunk, :]
            q_in = (q * jnp.exp(b)).astype(BF16)
            k_in = (k * jnp.exp(-b)).astype(BF16)
            k_out = (k * jnp.exp(b_last - b)).astype(BF16)
            v = iv.astype(BF16)
            v_t = iv.T.astype(BF16)
            attn = jnp.where(causal, _dot_nt(q_in, k_in), 0.0).astype(BF16)
            st = st_s[h]
            o = (jnp.dot(attn, v, preferred_element_type=F32)
                 + _dot_nt(q_in, st.astype(BF16)))
            st_s[h] = st * jnp.exp(b_last) + jnp.dot(v_t, k_out, preferred_element_type=F32)
            o = o * lax.rsqrt(jnp.mean(o * o, axis=-1, keepdims=True) + EPS)
            y = o * hgain[:, lo:lo + A_DK] * (gg * _sigmoid(gg))
            y_s[rows, pl.ds(lo, A_DK)] = y.astype(BF16)
        return carry

    lax.fori_loop(0, ts // chunk, chunk_body, 0)

    mix = jnp.dot(y_s[...], wout_ref[...], preferred_element_type=F32)
    out_ref[0] = x_ref[0] + _rms(mix, gpost_ref[...])

    @pl.when(t == pl.num_programs(1) - 1)
    def _():
        sfin_ref[0] = st_s[...]


def _hgrn2_layer(h, g_pre, g_post, lb_logits, head_gain, w_in, w_out, s0, *, ts, chunk, layer):
    bsz, t_len, d = h.shape
    assert t_len % ts == 0 and ts % chunk == 0
    row = lambda v: v.reshape(1, d)
    st_shape = (A_HEADS, A_DK, A_DK)
    out, s_fin = pl.pallas_call(
        functools.partial(_hgrn2_kernel, ts=ts, chunk=chunk, layer=layer),
        grid=(bsz, t_len // ts),
        in_specs=[
            pl.BlockSpec((1, ts, d), lambda b, t: (b, t, 0)),
            _const_spec((1, d)), _const_spec((1, d)), _const_spec(lb_logits.shape), _const_spec((1, d)),
            _const_spec(w_in.shape), _const_spec(w_out.shape), _const_spec(st_shape),
        ],
        out_specs=[
            pl.BlockSpec((1, ts, d), lambda b, t: (b, t, 0)),
            pl.BlockSpec((1,) + st_shape, lambda b, t: (b, 0, 0, 0)),
        ],
        out_shape=[jax.ShapeDtypeStruct(h.shape, F32),
                   jax.ShapeDtypeStruct((bsz,) + st_shape, F32)],
        scratch_shapes=[pltpu.VMEM((ts, 4 * d), F32), pltpu.VMEM((ts, d), BF16),
                        pltpu.VMEM(st_shape, F32)],
        compiler_params=_params(),
        name="hgrn2_mixer",
    )(h, row(g_pre), row(g_post), lb_logits, row(head_gain), w_in, w_out, s0)
    return out, s_fin


def _ffn_kernel(*refs, ts, with_attn):
    if with_attn:
        (h_ref, o_ref, wo_ref, gmix_ref, g2_ref, g3_ref, wup_ref, conv_ref, wdown_ref, tail0_ref,
         out_ref, tail_ref, u_s, act_s) = refs
    else:
        (h_ref, g2_ref, g3_ref, wup_ref, conv_ref, wdown_ref, tail0_ref,
         out_ref, tail_ref, u_s, act_s) = refs
    t = pl.program_id(1)
    halo = SUBLANES

    @pl.when(t == 0)
    def _():
        u_s[0:halo, :] = tail0_ref[...]

    h = h_ref[0]
    if with_attn:
        mix = jnp.dot(o_ref[0], wo_ref[...], preferred_element_type=F32)
        h = h + _rms(mix, gmix_ref[...])
    xn = _rms(h, g2_ref[...]).astype(BF16)
    u_s[halo:halo + ts, :] = jnp.dot(xn, wup_ref[...], preferred_element_type=F32)

    rb = min(CONV_ROWS, ts)

    def row_body(r, carry):
        r0 = pl.multiple_of(r * rb, rb)
        for j in range(D_FF_PAD // CONV_COLS):
            def conv(c0):
                blk = u_s[pl.ds(r0, rb + halo), pl.ds(c0, CONV_COLS)]
                w = conv_ref[:, pl.ds(c0, CONV_COLS)]
                return (w[0:1] * blk[halo - 2:halo - 2 + rb] + w[1:2] * blk[halo - 1:halo - 1 + rb]
                        + w[2:3] * blk[halo:halo + rb])
            cg = conv(j * CONV_COLS)
            cv = conv(D_FF_PAD + j * CONV_COLS)
            act_s[pl.ds(r0, rb), pl.ds(j * CONV_COLS, CONV_COLS)] = (cg * _sigmoid(cg) * cv).astype(BF16)
        return carry

    lax.fori_loop(0, ts // rb, row_body, 0)
    u_s[0:halo, :] = u_s[ts:ts + halo, :]

    ff = jnp.dot(act_s[...], wdown_ref[...], preferred_element_type=F32)
    out_ref[0] = h + _rms(ff, g3_ref[...])

    @pl.when(t == pl.num_programs(1) - 1)
    def _():
        tail_ref[0] = u_s[0:halo, :]


def _ffn_layer(h, g2, g3, w_up, conv_w, w_down, tail0, *, ts, attn=None):
    bsz, t_len, d = h.shape
    assert t_len % ts == 0
    row = lambda v: v.reshape(1, d)
    tile = pl.BlockSpec((1, ts, d), lambda b, t: (b, t, 0))
    args, specs = [h], [tile]
    if attn is not None:
        o, w_o, g_mix = attn
        args += [o, w_o, row(g_mix)]
        specs += [tile, _const_spec(w_o.shape), _const_spec((1, d))]
    args += [row(g2), row(g3), w_up, conv_w, w_down, tail0]
    specs += [_const_spec((1, d)), _const_spec((1, d)), _const_spec(w_up.shape), _const_spec(conv_w.shape),
              _const_spec(w_down.shape), _const_spec(tail0.shape)]
    out, tail = pl.pallas_call(
        functools.partial(_ffn_kernel, ts=ts, with_attn=attn is not None),
        grid=(bsz, t_len // ts),
        in_specs=specs,
        out_specs=[tile, pl.BlockSpec((1, SUBLANES, 2 * D_FF_PAD), lambda b, t: (b, 0, 0))],
        out_shape=[jax.ShapeDtypeStruct(h.shape, F32),
                   jax.ShapeDtypeStruct((bsz, SUBLANES, 2 * D_FF_PAD), F32)],
        scratch_shapes=[pltpu.VMEM((ts + SUBLANES, 2 * D_FF_PAD), F32), pltpu.VMEM((ts, D_FF_PAD), BF16)],
        compiler_params=_params(),
        name="conv_ffn",
    )(*args)
    return out, tail


def _fox_proj_kernel(h_ref, gq_ref, gkv_ref, wq_ref, wk_ref, wv_ref, wfh_ref, wfl_ref, fgb_ref, sel_ref,
                     c0_ref, q_out, k_out, v_out, cend_ref, q_s, k_s, v_s, aux_s, carry_s, *, ts):
    t = pl.program_id(1)

    @pl.when(t == 0)
    def _():
        carry_s[...] = c0_ref[...]

    h = h_ref[0]
    y = h * lax.rsqrt(jnp.mean(h * h, axis=-1, keepdims=True) + EPS)
    xq = (y * gq_ref[...]).astype(BF16)
    xkv = y * gkv_ref[...]
    xkv_hi = xkv.astype(BF16)
    xkv_lo = (xkv - xkv_hi.astype(F32)).astype(BF16)
    dot = functools.partial(jnp.dot, preferred_element_type=F32)
    q_s[...] = dot(xq, wq_ref[...]) * (1.0 / np.sqrt(B_HDIM).astype(np.float32))
    k_s[...] = dot(xkv_hi, wk_ref[...])
    v_s[...] = dot(xkv_hi, wv_ref[...])
    zf = (dot(xkv_hi, wfh_ref[...]) + dot(xkv_lo, wfh_ref[...]) + dot(xkv_hi, wfl_ref[...])
          + fgb_ref[...])
    ls = jnp.minimum(zf, 0.0) - jnp.log(1.0 + jnp.exp(-jnp.abs(zf)))
    c = carry_s[...] + _cumsum_rows(ls, _tril(ts).astype(BF16))
    carry_s[...] = c[ts - 1:ts, :]

    c_hi = c.astype(BF16).astype(F32)
    r = c - c_hi
    c_mid = r.astype(BF16).astype(F32)
    c_lo = (r - c_mid).astype(BF16).astype(F32)
    lane = lax.broadcasted_iota(jnp.int32, (ts, LANES), 1)
    parts = jnp.where(lane < B_HEADS, c_hi,
                      jnp.where(lane < 2 * B_HEADS, pltpu.roll(c_mid, B_HEADS, 1),
                                jnp.where(lane < 3 * B_HEADS, pltpu.roll(c_lo, 2 * B_HEADS, 1), 0.0)))
    aux_s[...] = dot(parts.astype(BF16), sel_ref[...])

    for hd in range(B_HEADS):
        slab = pl.ds((hd // 2) * LANES, LANES)
        def head(ref):
            x = ref[:, slab]
            return pltpu.roll(x, B_HDIM, 1) if hd % 2 else x
        a = aux_s[:, pl.ds(hd * ATT_PAD, ATT_PAD)]
        data = lane < B_HDIM
        plus = lane < B_HDIM + 3
        minus = lane < B_HDIM + 6
        q_out[0, hd] = jnp.where(data, head(q_s), jnp.where(plus, a, jnp.where(minus, 1.0, 0.0))).astype(BF16)
        k_out[0, hd] = jnp.where(data, head(k_s), jnp.where(plus, 1.0, jnp.where(minus, a, 0.0))).astype(BF16)
        v_out[0, hd] = jnp.where(data, head(v_s), jnp.where(lane == B_HDIM, 1.0, 0.0)).astype(BF16)

    @pl.when(t == pl.num_programs(1) - 1)
    def _():
        cend_ref[0] = carry_s[...]


def _fox_sel_matrix():
    sel = np.zeros((LANES, B_HEADS * ATT_PAD), np.float32)
    for term in range(3):
        for hd in range(B_HEADS):
            sel[term * B_HEADS + hd, hd * ATT_PAD + B_HDIM + term] = 1.0
            sel[term * B_HEADS + hd, hd * ATT_PAD + B_HDIM + 3 + term] = -1.0
    return jnp.asarray(sel, BF16)


def _fox_proj(h, g_q, g_kv, w_q, w_k, w_v, w_f_hi, w_f_lo, fg_b, c0, *, ts):
    bsz, t_len, d = h.shape
    assert t_len % ts == 0
    row = lambda v: v.reshape(1, d)
    sel = _fox_sel_matrix()
    head_shape = (bsz, B_HEADS, t_len, ATT_PAD)
    head_spec = pl.BlockSpec((1, B_HEADS, ts, ATT_PAD), lambda b, t: (b, 0, t, 0))
    return pl.pallas_call(
        functools.partial(_fox_proj_kernel, ts=ts),
        grid=(bsz, t_len // ts),
        in_specs=[
            pl.BlockSpec((1, ts, d), lambda b, t: (b, t, 0)),
            _const_spec((1, d)), _const_spec((1, d)),
            _const_spec(w_q.shape), _const_spec(w_k.shape), _const_spec(w_v.shape),
            _const_spec(w_f_hi.shape), _const_spec(w_f_lo.shape), _const_spec(fg_b.shape),
            _const_spec(sel.shape), _const_spec(c0.shape),
        ],
        out_specs=[head_spec, head_spec, head_spec,
                   pl.BlockSpec((1, 1, LANES), lambda b, t: (b, 0, 0))],
        out_shape=[jax.ShapeDtypeStruct(head_shape, BF16)] * 3 + [jax.ShapeDtypeStruct((bsz, 1, LANES), F32)],
        scratch_shapes=[pltpu.VMEM((ts, d), F32)] * 3
                       + [pltpu.VMEM((ts, B_HEADS * ATT_PAD), F32), pltpu.VMEM((1, LANES), F32)],
        compiler_params=_params(),
        name="fox_proj",
    )(h, row(g_q), row(g_kv), w_q, w_k, w_v, w_f_hi, w_f_lo, fg_b, sel, c0)


def _fox_attn_kernel(*refs, tq, hp, n_meta):
    if n_meta:
        q_ref, k_ref, v_ref, km_ref, vm_ref, o_ref, s_a, s_b, p_a, p_b = refs
        mp = km_ref.shape[2]
        meta_ok = lax.broadcasted_iota(jnp.int32, (tq, mp), 1) < n_meta
    else:
        q_ref, k_ref, v_ref, o_ref, s_a, s_b, p_a, p_b = refs
        mp = 0
    t_len = q_ref.shape[2]
    causal = _tril(tq)
    rs = min(tq, ATT_STRIP)
    unit = 0
    for i in range(t_len // tq):
        r0 = i * tq
        w = mp + r0 + tq
        for hh in range(hp):
            s_s, p_s = (s_a, p_a) if unit % 2 == 0 else (s_b, p_b)
            unit += 1
            q = q_ref[0, hh, r0:r0 + tq, :]
            if n_meta:
                s_s[:, 0:mp] = jnp.where(meta_ok, _dot_nt(q, km_ref[0, hh]), MASKED)
            if i > 0:
                s_s[:, mp:mp + r0] = _dot_nt(q, k_ref[0, hh, 0:r0, :])
            s_s[:, mp + r0:w] = jnp.where(causal, _dot_nt(q, k_ref[0, hh, r0:r0 + tq, :]), MASKED)
            for r in range(0, tq, rs):
                x = s_s[r:r + rs, 0:w]
                m = jnp.max(x, axis=-1, keepdims=True)
                p_s[r:r + rs, 0:w] = jnp.exp(x - m).astype(BF16)
            acc = jnp.dot(p_s[:, mp:w], v_ref[0, hh, 0:r0 + tq, :], preferred_element_type=F32)
            if n_meta:
                acc = acc + jnp.dot(p_s[:, 0:mp], vm_ref[0, hh], preferred_element_type=F32)
            o = acc[:, :B_HDIM] / acc[:, B_HDIM:B_HDIM + 1]
            o_ref[0, r0:r0 + tq, hh * B_HDIM:(hh + 1) * B_HDIM] = o.astype(BF16)


def _fox_attn(q, k, v, meta_kv, *, tq, hp=HEADS_PER_STEP):
    bsz, heads, t_len, pad = q.shape
    assert t_len % tq == 0 and heads % hp == 0
    head_spec = pl.BlockSpec((1, hp, t_len, pad), lambda b, g: (b, g, 0, 0))
    args, specs = [q, k, v], [head_spec] * 3
    n_meta = mp = 0
    if meta_kv is not None:
        n_meta = meta_kv[0].shape[2]
        mp = LANES
        args += [jnp.pad(a, ((0, 0), (0, 0), (0, mp - n_meta), (0, 0))) for a in meta_kv]
        specs += [pl.BlockSpec((1, hp, mp, pad), lambda b, g: (0, g, 0, 0))] * 2
    return pl.pallas_call(
        functools.partial(_fox_attn_kernel, tq=tq, hp=hp, n_meta=n_meta),
        grid=(bsz, heads // hp),
        in_specs=specs,
        out_specs=pl.BlockSpec((1, t_len, hp * B_HDIM), lambda b, g: (b, 0, g)),
        out_shape=jax.ShapeDtypeStruct((bsz, t_len, heads * B_HDIM), BF16),
        scratch_shapes=[pltpu.VMEM((tq, mp + t_len), F32)] * 2 + [pltpu.VMEM((tq, mp + t_len), BF16)] * 2,
        compiler_params=_params(),
        name="fox_attention",
    )(*args)


def _pad_cols(w, n):
    return jnp.pad(w, ((0, 0), (0, n - w.shape[1])))


def _prep_ffn(w_up, conv_w, w_down):
    up = jnp.concatenate([_pad_cols(w_up[:, :D_FF], D_FF_PAD), _pad_cols(w_up[:, D_FF:], D_FF_PAD)], axis=1)
    cw = jnp.concatenate([_pad_cols(conv_w[:, :D_FF], D_FF_PAD), _pad_cols(conv_w[:, D_FF:], D_FF_PAD)], axis=1)
    down = jnp.pad(w_down, ((0, D_FF_PAD - D_FF), (0, 0)))
    return up.astype(BF16), cw.astype(F32), down.astype(BF16)


def _trunk(h, params, carry_in, *, ts, chunk, tq):
    s0, tail_a, tail_b, c0, meta_kv = carry_in
    p = params
    h1, s_fin = _hgrn2_layer(h, p["g"][0, 0], p["g"][0, 1], p["lb_logits"], p["head_gain"], p["w_in"],
                             p["a_w_out"], s0, ts=ts, chunk=chunk, layer=0)
    h2, tail_a_out = _ffn_layer(h1, p["g"][0, 2], p["g"][0, 3], *p["ffn"][0], tail_a, ts=ts)
    q, k, v, c_end = _fox_proj(h2, p["g"][1, 0], p["kv_norm"], p["w_q"], p["w_k"], p["w_v"],
                               p["w_f_hi"], p["w_f_lo"], p["fg_b"], c0, ts=ts)
    o = _fox_attn(q, k, v, meta_kv, tq=tq)
    h4, tail_b_out = _ffn_layer(h2, p["g"][1, 2], p["g"][1, 3], *p["ffn"][1], tail_b, ts=ts,
                                attn=(o, p["b_w_out"], p["g"][1, 1]))
    return h4, (s_fin[0], tail_a_out[0], tail_b_out[0], c_end[0], (k[:1], v[:1]))


def kernel(x, meta_tokens, norm_gains, a_w_in, a_lb_logits, a_head_norm, a_w_out, kv_norm, kv_w, fg_b,
           b_w_q, b_w_out, ffn_w_up, ffn_conv, ffn_w_down):
    d = D_MODEL
    assert x.shape[2] == d and x.shape[1] % SEQ_TILE == 0 and meta_tokens.shape == (N_META, d)
    assert norm_gains.shape[0] == 2 and a_w_in.shape[0] == 1 and b_w_q.shape[0] == 1

    w_f = jnp.pad(kv_w[:, 2 * d:], ((0, 0), (0, LANES - B_HEADS))).astype(F32)
    w_f_hi = w_f.astype(BF16)
    params = {
        "g": norm_gains.astype(F32),
        "lb_logits": a_lb_logits.astype(F32),
        "head_gain": a_head_norm[0].astype(F32),
        "w_in": a_w_in[0].astype(BF16),
        "a_w_out": a_w_out[0].astype(BF16),
        "kv_norm": kv_norm.astype(F32),
        "w_q": b_w_q[0].astype(BF16),
        "w_k": kv_w[:, :d].astype(BF16),
        "w_v": kv_w[:, d:2 * d].astype(BF16),
        "w_f_hi": w_f_hi,
        "w_f_lo": (w_f - w_f_hi.astype(F32)).astype(BF16),
        "fg_b": jnp.pad(fg_b.astype(F32), (0, LANES - B_HEADS)).reshape(1, LANES),
        "b_w_out": b_w_out[0].astype(BF16),
        "ffn": [_prep_ffn(ffn_w_up[l], ffn_conv[l], ffn_w_down[l]) for l in range(2)],
    }

    zero_carry = (jnp.zeros((A_HEADS, A_DK, A_DK), F32),
                  jnp.zeros((SUBLANES, 2 * D_FF_PAD), F32), jnp.zeros((SUBLANES, 2 * D_FF_PAD), F32),
                  jnp.zeros((1, LANES), F32), None)
    _, meta_carry = _trunk(meta_tokens[None].astype(F32), params, zero_carry,
                           ts=N_META, chunk=N_META, tq=N_META)
    out, _ = _trunk(x, params, meta_carry, ts=SEQ_TILE, chunk=A_CHUNK, tq=Q_TILE)
    return out
```

```python
import functools

import numpy as np
import jax
import jax.numpy as jnp
from jax import lax
from jax.experimental import pallas as pl
from jax.experimental.pallas import tpu as pltpu

F32 = jnp.float32
BF16 = jnp.bfloat16

D_MODEL = 1024
N_META = 16
A_HEADS = 8
A_DK = D_MODEL // A_HEADS
A_CHUNK = 64
B_HEADS = 16
B_HDIM = D_MODEL // B_HEADS
D_FF = 2752
EPS = 1e-6

LANES = 128
SUBLANES = 8
MXU_DIM = 256
D_FF_PAD = -(-D_FF // MXU_DIM) * MXU_DIM
ATT_PAD = LANES
VMEM_LIMIT = 56 * 1024 * 1024

SEQ_TILE = 256
FFN_TILE = 512
Q_TILE = 256
GATE_COLS = MXU_DIM
CONV_ROWS = 64
CONV_COLS = 256
HEADS_PER_STEP = 2
ATT_STRIP = 16
MASKED = -1e30
LOG2E = 1.4426950408889634


def _rms(x, g):
    return x * lax.rsqrt(jnp.mean(x * x, axis=-1, keepdims=True) + EPS) * g


def _sigmoid(x):
    return 1.0 / (1.0 + jnp.exp(-x))


def _split3(x):
    hi = x.astype(BF16)
    r = x - hi.astype(F32)
    mid = r.astype(BF16)
    lo = (r - mid.astype(F32)).astype(BF16)
    return hi, mid, lo


def _tril(n):
    row = lax.broadcasted_iota(jnp.int32, (n, n), 0)
    col = lax.broadcasted_iota(jnp.int32, (n, n), 1)
    return row >= col


def _cumsum_rows(x, tril_bf16):
    hi, mid, lo = _split3(x)
    dot = functools.partial(jnp.dot, preferred_element_type=F32)
    return dot(tril_bf16, hi) + dot(tril_bf16, mid) + dot(tril_bf16, lo)


def _dot_nt(a, b):
    return lax.dot_general(a, b, (((1,), (1,)), ((), ())), preferred_element_type=F32)


def _const_spec(shape):
    nd = len(shape)
    return pl.BlockSpec(shape, lambda *_: (0,) * nd, pipeline_mode=pl.Buffered(1))


def _params():
    return pltpu.CompilerParams(dimension_semantics=("arbitrary", "arbitrary"),
                                vmem_limit_bytes=VMEM_LIMIT)


def _hgrn2_kernel(x_ref, gpre_ref, gpost_ref, lbl_ref, hg_ref, win_ref, wout_ref, s0_ref,
                  out_ref, sfin_ref, proj_s, qin_s, kin_s, kout_s, v_s, dec_s, y_s, st_s, *, ts, chunk, layer):
    t = pl.program_id(1)
    n_chunks = ts // chunk

    @pl.when(t == 0)
    def _():
        st_s[...] = s0_ref[...]

    lbl = lbl_ref[...]
    e = jnp.exp(lbl - jnp.max(lbl, axis=0, keepdims=True))
    lb = jnp.sum(e[:layer + 1], axis=0, keepdims=True) / jnp.sum(e, axis=0, keepdims=True)

    xn = _rms(x_ref[0], gpre_ref[...]).astype(BF16)
    proj_s[...] = jnp.dot(xn, win_ref[...], preferred_element_type=F32)

    tril3 = jnp.concatenate([_tril(chunk).astype(BF16)] * 3, axis=1)
    for c in range(n_chunks):
        rows = slice(c * chunk, (c + 1) * chunk)
        for c0 in range(0, D_MODEL, GATE_COLS):
            cols = slice(c0, c0 + GATE_COLS)
            lbc = lb[:, cols]
            f = lbc + (1.0 - lbc) * _sigmoid(proj_s[rows, D_MODEL + c0:D_MODEL + c0 + GATE_COLS])
            k = 1.0 - f
            b = jnp.dot(tril3, jnp.concatenate(_split3(jnp.log(f)), axis=0), preferred_element_type=F32)
            dec = jnp.exp(b[chunk - 1:chunk, :])
            k_in = k * jnp.exp(-b)
            qin_s[rows, cols] = (proj_s[rows, cols] * jnp.exp(b)).astype(BF16)
            kin_s[rows, cols] = k_in.astype(BF16)
            kout_s[rows, cols] = (k_in * dec).astype(BF16)
            v_s[rows, cols] = proj_s[rows, 2 * D_MODEL + c0:2 * D_MODEL + c0 + GATE_COLS].astype(BF16)
            dec_s[c:c + 1, cols] = dec

    row_c = lax.broadcasted_iota(jnp.int32, (ts, ts), 0)
    col_c = lax.broadcasted_iota(jnp.int32, (ts, ts), 1)
    intra = (row_c >= col_c) & (row_c // chunk == col_c // chunk)
    vt_chunk = lax.broadcasted_iota(jnp.int32, (A_DK, ts), 1) // chunk
    hgain = hg_ref[...]
    for h in range(A_HEADS):
        cols = slice(h * A_DK, (h + 1) * A_DK)
        q_in = qin_s[:, cols]
        attn = jnp.where(intra, _dot_nt(q_in, kin_s[:, cols]), 0.0).astype(BF16)
        o_intra = jnp.dot(attn, v_s[:, cols], preferred_element_type=F32)
        v_t = proj_s[:, 2 * D_MODEL + h * A_DK:2 * D_MODEL + (h + 1) * A_DK].T.astype(BF16)
        v_t_blocks = jnp.concatenate([jnp.where(vt_chunk == c, v_t, jnp.zeros_like(v_t))
                                      for c in range(n_chunks)], axis=0)
        d_st = jnp.dot(v_t_blocks, kout_s[:, cols], preferred_element_type=F32)
        st = st_s[h]
        o_inter = []
        for c in range(n_chunks):
            o_inter.append(_dot_nt(q_in[c * chunk:(c + 1) * chunk], st.astype(BF16)))
            st = st * dec_s[c:c + 1, cols] + d_st[c * A_DK:(c + 1) * A_DK]
        st_s[h] = st
        o = o_intra + jnp.concatenate(o_inter, axis=0)
        o = o * lax.rsqrt(jnp.mean(o * o, axis=-1, keepdims=True) + EPS)
        gg = proj_s[:, 3 * D_MODEL + h * A_DK:3 * D_MODEL + (h + 1) * A_DK]
        y_s[:, cols] = (o * hgain[:, cols] * (gg * _sigmoid(gg))).astype(BF16)

    mix = jnp.dot(y_s[...], wout_ref[...], preferred_element_type=F32)
    out_ref[0] = x_ref[0] + _rms(mix, gpost_ref[...])

    @pl.when(t == pl.num_programs(1) - 1)
    def _():
        sfin_ref[0] = st_s[...]


def _hgrn2_layer(h, g_pre, g_post, lb_logits, head_gain, w_in, w_out, s0, *, ts, chunk, layer):
    bsz, t_len, d = h.shape
    assert t_len % ts == 0 and ts % chunk == 0
    row = lambda v: v.reshape(1, d)
    st_shape = (A_HEADS, A_DK, A_DK)
    out, s_fin = pl.pallas_call(
        functools.partial(_hgrn2_kernel, ts=ts, chunk=chunk, layer=layer),
        grid=(bsz, t_len // ts),
        in_specs=[
            pl.BlockSpec((1, ts, d), lambda b, t: (b, t, 0)),
            _const_spec((1, d)), _const_spec((1, d)), _const_spec(lb_logits.shape), _const_spec((1, d)),
            _const_spec(w_in.shape), _const_spec(w_out.shape), _const_spec(st_shape),
        ],
        out_specs=[
            pl.BlockSpec((1, ts, d), lambda b, t: (b, t, 0)),
            pl.BlockSpec((1,) + st_shape, lambda b, t: (b, 0, 0, 0)),
        ],
        out_shape=[jax.ShapeDtypeStruct(h.shape, F32),
                   jax.ShapeDtypeStruct((bsz,) + st_shape, F32)],
        scratch_shapes=[pltpu.VMEM((ts, 4 * d), F32)] + [pltpu.VMEM((ts, d), BF16)] * 4
                       + [pltpu.VMEM((max(ts // chunk, SUBLANES), d), F32), pltpu.VMEM((ts, d), BF16),
                          pltpu.VMEM(st_shape, F32)],
        compiler_params=_params(),
        name="hgrn2_mixer",
    )(h, row(g_pre), row(g_post), lb_logits, row(head_gain), w_in, w_out, s0)
    return out, s_fin


def _ffn_kernel(*refs, ts, with_attn):
    if with_attn:
        (h_ref, o_ref, wo_ref, gmix_ref, g2_ref, g3_ref, wup_ref, conv_ref, wdown_ref, tail0_ref,
         out_ref, tail_ref, u_s, act_s) = refs
    else:
        (h_ref, g2_ref, g3_ref, wup_ref, conv_ref, wdown_ref, tail0_ref,
         out_ref, tail_ref, u_s, act_s) = refs
    t = pl.program_id(1)
    halo = SUBLANES

    @pl.when(t == 0)
    def _():
        u_s[0:halo, :] = tail0_ref[...]

    h = h_ref[0]
    if with_attn:
        mix = jnp.dot(o_ref[0], wo_ref[...], preferred_element_type=F32)
        h = h + _rms(mix, gmix_ref[...])
    xn = _rms(h, g2_ref[...]).astype(BF16)
    rb = min(CONV_ROWS, ts)
    for j in range(D_FF_PAD // CONV_COLS):
        starts = (j * CONV_COLS, D_FF_PAD + j * CONV_COLS)
        for c0 in starts:
            u_s[halo:halo + ts, c0:c0 + CONV_COLS] = jnp.dot(
                xn, wup_ref[:, c0:c0 + CONV_COLS], preferred_element_type=F32)
        for r0 in range(0, ts, rb):
            def conv(c0):
                w = conv_ref[:, c0:c0 + CONV_COLS]
                tap = lambda d: u_s[halo + r0 - d:halo + r0 - d + rb, c0:c0 + CONV_COLS]
                return w[0:1] * tap(2) + w[1:2] * tap(1) + w[2:3] * tap(0)
            cg, cv = conv(starts[0]), conv(starts[1])
            act_s[r0:r0 + rb, j * CONV_COLS:(j + 1) * CONV_COLS] = (cg * _sigmoid(cg) * cv).astype(BF16)
    u_s[0:halo, :] = u_s[ts:ts + halo, :]

    ff = jnp.dot(act_s[...], wdown_ref[...], preferred_element_type=F32)
    out_ref[0] = h + _rms(ff, g3_ref[...])

    @pl.when(t == pl.num_programs(1) - 1)
    def _():
        tail_ref[0] = u_s[0:halo, :]


def _ffn_layer(h, g2, g3, w_up, conv_w, w_down, tail0, *, ts, attn=None):
    bsz, t_len, d = h.shape
    assert t_len % ts == 0
    row = lambda v: v.reshape(1, d)
    tile = pl.BlockSpec((1, ts, d), lambda b, t: (b, t, 0))
    args, specs = [h], [tile]
    if attn is not None:
        o, w_o, g_mix = attn
        args += [o, w_o, row(g_mix)]
        specs += [tile, _const_spec(w_o.shape), _const_spec((1, d))]
    args += [row(g2), row(g3), w_up, conv_w, w_down, tail0]
    specs += [_const_spec((1, d)), _const_spec((1, d)), _const_spec(w_up.shape), _const_spec(conv_w.shape),
              _const_spec(w_down.shape), _const_spec(tail0.shape)]
    out, tail = pl.pallas_call(
        functools.partial(_ffn_kernel, ts=ts, with_attn=attn is not None),
        grid=(bsz, t_len // ts),
        in_specs=specs,
        out_specs=[tile, pl.BlockSpec((1, SUBLANES, 2 * D_FF_PAD), lambda b, t: (b, 0, 0))],
        out_shape=[jax.ShapeDtypeStruct(h.shape, F32),
                   jax.ShapeDtypeStruct((bsz, SUBLANES, 2 * D_FF_PAD), F32)],
        scratch_shapes=[pltpu.VMEM((ts + SUBLANES, 2 * D_FF_PAD), F32), pltpu.VMEM((ts, D_FF_PAD), BF16)],
        compiler_params=_params(),
        name="conv_ffn",
    )(*args)
    return out, tail


def _fox_proj_kernel(h_ref, gq_ref, gkv_ref, wq_ref, wk_ref, wv_ref, wfh_ref, wfl_ref, fgb_ref, sel_ref,
                     c0_ref, q_out, k_out, v_out, cend_ref, q_s, k_s, v_s, aux_s, carry_s, *, ts):
    t = pl.program_id(1)

    @pl.when(t == 0)
    def _():
        carry_s[...] = c0_ref[...]

    h = h_ref[0]
    y = h * lax.rsqrt(jnp.mean(h * h, axis=-1, keepdims=True) + EPS)
    xq = (y * gq_ref[...]).astype(BF16)
    xkv = y * gkv_ref[...]
    xkv_hi = xkv.astype(BF16)
    xkv_lo = (xkv - xkv_hi.astype(F32)).astype(BF16)
    dot = functools.partial(jnp.dot, preferred_element_type=F32)
    q_s[...] = dot(xq, wq_ref[...]) * np.float32(LOG2E / np.sqrt(B_HDIM))
    k_s[...] = dot(xkv_hi, wk_ref[...])
    v_s[...] = dot(xkv_hi, wv_ref[...])
    zf = (dot(xkv_hi, wfh_ref[...]) + dot(xkv_lo, wfh_ref[...]) + dot(xkv_hi, wfl_ref[...])
          + fgb_ref[...])
    ls = jnp.minimum(zf, 0.0) - jnp.log(1.0 + jnp.exp(-jnp.abs(zf)))
    c = carry_s[...] + _cumsum_rows(ls, _tril(ts).astype(BF16))
    carry_s[...] = c[ts - 1:ts, :]

    c2 = c * np.float32(LOG2E)
    c_hi = c2.astype(BF16).astype(F32)
    r = c2 - c_hi
    c_mid = r.astype(BF16).astype(F32)
    c_lo = (r - c_mid).astype(BF16).astype(F32)
    lane = lax.broadcasted_iota(jnp.int32, (ts, LANES), 1)
    parts = jnp.where(lane < B_HEADS, c_hi,
                      jnp.where(lane < 2 * B_HEADS, pltpu.roll(c_mid, B_HEADS, 1),
                                jnp.where(lane < 3 * B_HEADS, pltpu.roll(c_lo, 2 * B_HEADS, 1), 0.0)))
    aux_s[...] = dot(parts.astype(BF16), sel_ref[...])

    for hd in range(B_HEADS):
        slab = pl.ds((hd // 2) * LANES, LANES)
        def head(ref):
            x = ref[:, slab]
            return pltpu.roll(x, B_HDIM, 1) if hd % 2 else x
        a = aux_s[:, pl.ds(hd * ATT_PAD, ATT_PAD)]
        data = lane < B_HDIM
        plus = lane < B_HDIM + 3
        minus = lane < B_HDIM + 6
        q_out[0, hd] = jnp.where(data, head(q_s), jnp.where(plus, a, jnp.where(minus, 1.0, 0.0))).astype(BF16)
        k_out[0, hd] = jnp.where(data, head(k_s), jnp.where(plus, 1.0, jnp.where(minus, a, 0.0))).astype(BF16)
        v_out[0, hd] = jnp.where(data, head(v_s), jnp.where(lane == B_HDIM, 1.0, 0.0)).astype(BF16)

    @pl.when(t == pl.num_programs(1) - 1)
    def _():
        cend_ref[0] = carry_s[...]


def _fox_sel_matrix():
    sel = np.zeros((LANES, B_HEADS * ATT_PAD), np.float32)
    for term in range(3):
        for hd in range(B_HEADS):
            sel[term * B_HEADS + hd, hd * ATT_PAD + B_HDIM + term] = 1.0
            sel[term * B_HEADS + hd, hd * ATT_PAD + B_HDIM + 3 + term] = -1.0
    return jnp.asarray(sel, BF16)


def _fox_proj(h, g_q, g_kv, w_q, w_k, w_v, w_f_hi, w_f_lo, fg_b, c0, *, ts):
    bsz, t_len, d = h.shape
    assert t_len % ts == 0
    row = lambda v: v.reshape(1, d)
    sel = _fox_sel_matrix()
    head_shape = (bsz, B_HEADS, t_len, ATT_PAD)
    head_spec = pl.BlockSpec((1, B_HEADS, ts, ATT_PAD), lambda b, t: (b, 0, t, 0))
    return pl.pallas_call(
        functools.partial(_fox_proj_kernel, ts=ts),
        grid=(bsz, t_len // ts),
        in_specs=[
            pl.BlockSpec((1, ts, d), lambda b, t: (b, t, 0)),
            _const_spec((1, d)), _const_spec((1, d)),
            _const_spec(w_q.shape), _const_spec(w_k.shape), _const_spec(w_v.shape),
            _const_spec(w_f_hi.shape), _const_spec(w_f_lo.shape), _const_spec(fg_b.shape),
            _const_spec(sel.shape), _const_spec(c0.shape),
        ],
        out_specs=[head_spec, head_spec, head_spec,
                   pl.BlockSpec((1, 1, LANES), lambda b, t: (b, 0, 0))],
        out_shape=[jax.ShapeDtypeStruct(head_shape, BF16)] * 3 + [jax.ShapeDtypeStruct((bsz, 1, LANES), F32)],
        scratch_shapes=[pltpu.VMEM((ts, d), F32)] * 3
                       + [pltpu.VMEM((ts, B_HEADS * ATT_PAD), F32), pltpu.VMEM((1, LANES), F32)],
        compiler_params=_params(),
        name="fox_proj",
    )(h, row(g_q), row(g_kv), w_q, w_k, w_v, w_f_hi, w_f_lo, fg_b, sel, c0)


def _fox_attn_kernel(*refs, tq, hp, n_meta):
    if n_meta:
        q_ref, k_ref, v_ref, km_ref, vm_ref, o_ref, s_a, s_b, p_a, p_b = refs
        mp = km_ref.shape[2]
        meta_ok = lax.broadcasted_iota(jnp.int32, (tq, mp), 1) < n_meta
    else:
        q_ref, k_ref, v_ref, o_ref, s_a, s_b, p_a, p_b = refs
        mp = 0
    t_len = q_ref.shape[2]
    causal = _tril(tq)
    rs = min(tq, ATT_STRIP)
    unit = 0
    for i in range(t_len // tq):
        r0 = i * tq
        w = mp + r0 + tq
        for hh in range(hp):
            s_s, p_s = (s_a, p_a) if unit % 2 == 0 else (s_b, p_b)
            unit += 1
            q = q_ref[0, hh, r0:r0 + tq, :]
            if n_meta:
                s_s[:, 0:mp] = jnp.where(meta_ok, _dot_nt(q, km_ref[0, hh]), MASKED)
            if i > 0:
                s_s[:, mp:mp + r0] = _dot_nt(q, k_ref[0, hh, 0:r0, :])
            s_s[:, mp + r0:w] = jnp.where(causal, _dot_nt(q, k_ref[0, hh, r0:r0 + tq, :]), MASKED)
            for r in range(0, tq, rs):
                x = s_s[r:r + rs, 0:w]
                m = jnp.max(x, axis=-1, keepdims=True)
                p_s[r:r + rs, 0:w] = jnp.exp2(x - m).astype(BF16)
            acc = jnp.dot(p_s[:, mp:w], v_ref[0, hh, 0:r0 + tq, :], preferred_element_type=F32)
            if n_meta:
                acc = acc + jnp.dot(p_s[:, 0:mp], vm_ref[0, hh], preferred_element_type=F32)
            o = acc[:, :B_HDIM] / acc[:, B_HDIM:B_HDIM + 1]
            o_ref[0, r0:r0 + tq, hh * B_HDIM:(hh + 1) * B_HDIM] = o.astype(BF16)


def _fox_attn(q, k, v, meta_kv, *, tq, hp=HEADS_PER_STEP):
    bsz, heads, t_len, pad = q.shape
    assert t_len % tq == 0 and heads % hp == 0
    head_spec = pl.BlockSpec((1, hp, t_len, pad), lambda b, g: (b, g, 0, 0))
    args, specs = [q, k, v], [head_spec] * 3
    n_meta = mp = 0
    if meta_kv is not None:
        n_meta = meta_kv[0].shape[2]
        mp = LANES
        args += [jnp.pad(a, ((0, 0), (0, 0), (0, mp - n_meta), (0, 0))) for a in meta_kv]
        specs += [pl.BlockSpec((1, hp, mp, pad), lambda b, g: (0, g, 0, 0))] * 2
    return pl.pallas_call(
        functools.partial(_fox_attn_kernel, tq=tq, hp=hp, n_meta=n_meta),
        grid=(bsz, heads // hp),
        in_specs=specs,
        out_specs=pl.BlockSpec((1, t_len, hp * B_HDIM), lambda b, g: (b, 0, g)),
        out_shape=jax.ShapeDtypeStruct((bsz, t_len, heads * B_HDIM), BF16),
        scratch_shapes=[pltpu.VMEM((tq, mp + t_len), F32)] * 2 + [pltpu.VMEM((tq, mp + t_len), BF16)] * 2,
        compiler_params=_params(),
        name="fox_attention",
    )(*args)


def _pad_cols(w, n):
    return jnp.pad(w, ((0, 0), (0, n - w.shape[1])))


def _prep_ffn(w_up, conv_w, w_down):
    up = jnp.concatenate([_pad_cols(w_up[:, :D_FF], D_FF_PAD), _pad_cols(w_up[:, D_FF:], D_FF_PAD)], axis=1)
    cw = jnp.concatenate([_pad_cols(conv_w[:, :D_FF], D_FF_PAD), _pad_cols(conv_w[:, D_FF:], D_FF_PAD)], axis=1)
    down = jnp.pad(w_down, ((0, D_FF_PAD - D_FF), (0, 0)))
    return up.astype(BF16), cw.astype(F32), down.astype(BF16)


def _trunk(h, params, carry_in, *, ts, ts_ffn, chunk, tq):
    s0, tail_a, tail_b, c0, meta_kv = carry_in
    p = params
    h1, s_fin = _hgrn2_layer(h, p["g"][0, 0], p["g"][0, 1], p["lb_logits"], p["head_gain"], p["w_in"],
                             p["a_w_out"], s0, ts=ts, chunk=chunk, layer=0)
    h2, tail_a_out = _ffn_layer(h1, p["g"][0, 2], p["g"][0, 3], *p["ffn"][0], tail_a, ts=ts_ffn)
    q, k, v, c_end = _fox_proj(h2, p["g"][1, 0], p["kv_norm"], p["w_q"], p["w_k"], p["w_v"],
                               p["w_f_hi"], p["w_f_lo"], p["fg_b"], c0, ts=ts)
    o = _fox_attn(q, k, v, meta_kv, tq=tq)
    h4, tail_b_out = _ffn_layer(h2, p["g"][1, 2], p["g"][1, 3], *p["ffn"][1], tail_b, ts=ts_ffn,
                                attn=(o, p["b_w_out"], p["g"][1, 1]))
    return h4, (s_fin[0], tail_a_out[0], tail_b_out[0], c_end[0], (k[:1], v[:1]))


def kernel(x, meta_tokens, norm_gains, a_w_in, a_lb_logits, a_head_norm, a_w_out, kv_norm, kv_w, fg_b,
           b_w_q, b_w_out, ffn_w_up, ffn_conv, ffn_w_down):
    d = D_MODEL
    assert x.shape[2] == d and x.shape[1] % SEQ_TILE == 0 and meta_tokens.shape == (N_META, d)
    assert norm_gains.shape[0] == 2 and a_w_in.shape[0] == 1 and b_w_q.shape[0] == 1

    w_f = jnp.pad(kv_w[:, 2 * d:], ((0, 0), (0, LANES - B_HEADS))).astype(F32)
    w_f_hi = w_f.astype(BF16)
    params = {
        "g": norm_gains.astype(F32),
        "lb_logits": a_lb_logits.astype(F32),
        "head_gain": a_head_norm[0].astype(F32),
        "w_in": a_w_in[0].astype(BF16),
        "a_w_out": a_w_out[0].astype(BF16),
        "kv_norm": kv_norm.astype(F32),
        "w_q": b_w_q[0].astype(BF16),
        "w_k": kv_w[:, :d].astype(BF16),
        "w_v": kv_w[:, d:2 * d].astype(BF16),
        "w_f_hi": w_f_hi,
        "w_f_lo": (w_f - w_f_hi.astype(F32)).astype(BF16),
        "fg_b": jnp.pad(fg_b.astype(F32), (0, LANES - B_HEADS)).reshape(1, LANES),
        "b_w_out": b_w_out[0].astype(BF16),
        "ffn": [_prep_ffn(ffn_w_up[l], ffn_conv[l], ffn_w_down[l]) for l in range(2)],
    }

    zero_carry = (jnp.zeros((A_HEADS, A_DK, A_DK), F32),
                  jnp.zeros((SUBLANES, 2 * D_FF_PAD), F32), jnp.zeros((SUBLANES, 2 * D_FF_PAD), F32),
                  jnp.zeros((1, LANES), F32), None)
    _, meta_carry = _trunk(meta_tokens[None].astype(F32), params, zero_carry,
                           ts=N_META, ts_ffn=N_META, chunk=N_META, tq=N_META)
    out, _ = _trunk(x, params, meta_carry, ts=SEQ_TILE, ts_ffn=FFN_TILE, chunk=A_CHUNK, tq=Q_TILE)
    return out
```

```python
import functools

import numpy as np
import jax
import jax.numpy as jnp
from jax import lax
from jax.experimental import pallas as pl
from jax.experimental.pallas import tpu as pltpu

F32 = jnp.float32
BF16 = jnp.bfloat16

D_MODEL = 1024
N_META = 16
A_HEADS = 8
A_DK = D_MODEL // A_HEADS
A_CHUNK = 64
B_HEADS = 16
B_HDIM = D_MODEL // B_HEADS
D_FF = 2752
EPS = 1e-6

LANES = 128
SUBLANES = 8
MXU_DIM = 256
D_FF_PAD = -(-D_FF // MXU_DIM) * MXU_DIM
ATT_PAD = LANES
VMEM_LIMIT = 56 * 1024 * 1024

SEQ_TILE = 256
FFN_TILE = 512
Q_TILE = 256
PROJ_COLS = MXU_DIM
GATE_COLS = MXU_DIM
CONV_ROWS = 64
CONV_COLS = 256
HEADS_PER_STEP = 2
ATT_STRIP = 16
MASKED = -1e30
LOG2E = 1.4426950408889634


def _rms(x, g):
    return x * lax.rsqrt(jnp.mean(x * x, axis=-1, keepdims=True) + EPS) * g


def _sigmoid(x):
    return 1.0 / (1.0 + jnp.exp(-x))


def _split3(x):
    hi = x.astype(BF16)
    r = x - hi.astype(F32)
    mid = r.astype(BF16)
    lo = (r - mid.astype(F32)).astype(BF16)
    return hi, mid, lo


def _tril(n):
    row = lax.broadcasted_iota(jnp.int32, (n, n), 0)
    col = lax.broadcasted_iota(jnp.int32, (n, n), 1)
    return row >= col


def _cumsum_rows(x, tril_bf16):
    hi, mid, lo = _split3(x)
    dot = functools.partial(jnp.dot, preferred_element_type=F32)
    return dot(tril_bf16, hi) + dot(tril_bf16, mid) + dot(tril_bf16, lo)


def _dot_nt(a, b):
    return lax.dot_general(a, b, (((1,), (1,)), ((), ())), preferred_element_type=F32)


def _const_spec(shape):
    nd = len(shape)
    return pl.BlockSpec(shape, lambda *_: (0,) * nd, pipeline_mode=pl.Buffered(1))


def _params():
    return pltpu.CompilerParams(dimension_semantics=("arbitrary", "arbitrary"),
                                vmem_limit_bytes=VMEM_LIMIT)


def _hgrn2_kernel(x_ref, gpre_ref, gpost_ref, lbl_ref, hg_ref, win_ref, wout_ref, s0_ref,
                  out_ref, sfin_ref, proj_s, qin_s, kin_s, kout_s, v_s, dec_s, y_s, st_s, *, ts, chunk, layer):
    t = pl.program_id(1)
    n_chunks = ts // chunk

    @pl.when(t == 0)
    def _():
        st_s[...] = s0_ref[...]

    lbl = lbl_ref[...]
    e = jnp.exp(lbl - jnp.max(lbl, axis=0, keepdims=True))
    lb = jnp.sum(e[:layer + 1], axis=0, keepdims=True) / jnp.sum(e, axis=0, keepdims=True)

    xn = _rms(x_ref[0], gpre_ref[...]).astype(BF16)
    proj_s[...] = jnp.dot(xn, win_ref[...], preferred_element_type=F32)

    tril3 = jnp.concatenate([_tril(chunk).astype(BF16)] * 3, axis=1)
    for c in range(n_chunks):
        rows = slice(c * chunk, (c + 1) * chunk)
        for c0 in range(0, D_MODEL, GATE_COLS):
            cols = slice(c0, c0 + GATE_COLS)
            lbc = lb[:, cols]
            f = lbc + (1.0 - lbc) * _sigmoid(proj_s[rows, D_MODEL + c0:D_MODEL + c0 + GATE_COLS])
            k = 1.0 - f
            b = jnp.dot(tril3, jnp.concatenate(_split3(jnp.log(f)), axis=0), preferred_element_type=F32)
            dec = jnp.exp(b[chunk - 1:chunk, :])
            k_in = k * jnp.exp(-b)
            qin_s[rows, cols] = (proj_s[rows, cols] * jnp.exp(b)).astype(BF16)
            kin_s[rows, cols] = k_in.astype(BF16)
            kout_s[rows, cols] = (k_in * dec).astype(BF16)
            v_s[rows, cols] = proj_s[rows, 2 * D_MODEL + c0:2 * D_MODEL + c0 + GATE_COLS].astype(BF16)
            dec_s[c:c + 1, cols] = dec

    row_c = lax.broadcasted_iota(jnp.int32, (ts, ts), 0)
    col_c = lax.broadcasted_iota(jnp.int32, (ts, ts), 1)
    intra = (row_c >= col_c) & (row_c // chunk == col_c // chunk)
    vt_chunk = lax.broadcasted_iota(jnp.int32, (A_DK, ts), 1) // chunk
    hgain = hg_ref[...]
    for h in range(A_HEADS):
        cols = slice(h * A_DK, (h + 1) * A_DK)
        q_in = qin_s[:, cols]
        attn = jnp.where(intra, _dot_nt(q_in, kin_s[:, cols]), 0.0).astype(BF16)
        o_intra = jnp.dot(attn, v_s[:, cols], preferred_element_type=F32)
        v_t = proj_s[:, 2 * D_MODEL + h * A_DK:2 * D_MODEL + (h + 1) * A_DK].T.astype(BF16)
        v_t_blocks = jnp.concatenate([jnp.where(vt_chunk == c, v_t, jnp.zeros_like(v_t))
                                      for c in range(n_chunks)], axis=0)
        d_st = jnp.dot(v_t_blocks, kout_s[:, cols], preferred_element_type=F32)
        st = st_s[h]
        o_inter = []
        for c in range(n_chunks):
            o_inter.append(_dot_nt(q_in[c * chunk:(c + 1) * chunk], st.astype(BF16)))
            st = st * dec_s[c:c + 1, cols] + d_st[c * A_DK:(c + 1) * A_DK]
        st_s[h] = st
        o = o_intra + jnp.concatenate(o_inter, axis=0)
        o = o * lax.rsqrt(jnp.mean(o * o, axis=-1, keepdims=True) + EPS)
        gg = proj_s[:, 3 * D_MODEL + h * A_DK:3 * D_MODEL + (h + 1) * A_DK]
        y_s[:, cols] = (o * hgain[:, cols] * (gg * _sigmoid(gg))).astype(BF16)

    mix = jnp.dot(y_s[...], wout_ref[...], preferred_element_type=F32)
    out_ref[0] = x_ref[0] + _rms(mix, gpost_ref[...])

    @pl.when(t == pl.num_programs(1) - 1)
    def _():
        sfin_ref[0] = st_s[...]


def _hgrn2_layer(h, g_pre, g_post, lb_logits, head_gain, w_in, w_out, s0, *, ts, chunk, layer):
    bsz, t_len, d = h.shape
    assert t_len % ts == 0 and ts % chunk == 0
    row = lambda v: v.reshape(1, d)
    st_shape = (A_HEADS, A_DK, A_DK)
    out, s_fin = pl.pallas_call(
        functools.partial(_hgrn2_kernel, ts=ts, chunk=chunk, layer=layer),
        grid=(bsz, t_len // ts),
        in_specs=[
            pl.BlockSpec((1, ts, d), lambda b, t: (b, t, 0)),
            _const_spec((1, d)), _const_spec((1, d)), _const_spec(lb_logits.shape), _const_spec((1, d)),
            _const_spec(w_in.shape), _const_spec(w_out.shape), _const_spec(st_shape),
        ],
        out_specs=[
            pl.BlockSpec((1, ts, d), lambda b, t: (b, t, 0)),
            pl.BlockSpec((1,) + st_shape, lambda b, t: (b, 0, 0, 0)),
        ],
        out_shape=[jax.ShapeDtypeStruct(h.shape, F32),
                   jax.ShapeDtypeStruct((bsz,) + st_shape, F32)],
        scratch_shapes=[pltpu.VMEM((ts, 4 * d), F32)] + [pltpu.VMEM((ts, d), BF16)] * 4
                       + [pltpu.VMEM((max(ts // chunk, SUBLANES), d), F32), pltpu.VMEM((ts, d), BF16),
                          pltpu.VMEM(st_shape, F32)],
        compiler_params=_params(),
        name="hgrn2_mixer",
    )(h, row(g_pre), row(g_post), lb_logits, row(head_gain), w_in, w_out, s0)
    return out, s_fin


def _ffn_kernel(*refs, ts, with_attn):
    if with_attn:
        (h_ref, o_ref, wo_ref, gmix_ref, g2_ref, g3_ref, wup_ref, conv_ref, wdown_ref, tail0_ref,
         out_ref, tail_ref, u_s, act_s) = refs
    else:
        (h_ref, g2_ref, g3_ref, wup_ref, conv_ref, wdown_ref, tail0_ref,
         out_ref, tail_ref, u_s, act_s) = refs
    t = pl.program_id(1)
    halo = SUBLANES

    @pl.when(t == 0)
    def _():
        u_s[0:halo, :] = tail0_ref[...]

    h = h_ref[0]
    if with_attn:
        mix = jnp.dot(o_ref[0], wo_ref[...], preferred_element_type=F32)
        h = h + _rms(mix, gmix_ref[...])
    xn = _rms(h, g2_ref[...]).astype(BF16)
    rb = min(CONV_ROWS, ts)
    for j in range(D_FF_PAD // CONV_COLS):
        starts = (j * CONV_COLS, D_FF_PAD + j * CONV_COLS)
        for c0 in starts:
            u_s[halo:halo + ts, c0:c0 + CONV_COLS] = jnp.dot(
                xn, wup_ref[:, c0:c0 + CONV_COLS], preferred_element_type=F32)
        for r0 in range(0, ts, rb):
            def conv(c0):
                w = conv_ref[:, c0:c0 + CONV_COLS]
                blk = u_s[r0:r0 + rb + halo, c0:c0 + CONV_COLS]
                tap = lambda d: (pltpu.roll(blk, d, 0) if d else blk)[halo:halo + rb]
                return w[0:1] * tap(2) + w[1:2] * tap(1) + w[2:3] * tap(0)
            cg, cv = conv(starts[0]), conv(starts[1])
            act_s[r0:r0 + rb, j * CONV_COLS:(j + 1) * CONV_COLS] = (cg * _sigmoid(cg) * cv).astype(BF16)
    u_s[0:halo, :] = u_s[ts:ts + halo, :]

    ff = jnp.dot(act_s[...], wdown_ref[...], preferred_element_type=F32)
    out_ref[0] = h + _rms(ff, g3_ref[...])

    @pl.when(t == pl.num_programs(1) - 1)
    def _():
        tail_ref[0] = u_s[0:halo, :]


def _ffn_layer(h, g2, g3, w_up, conv_w, w_down, tail0, *, ts, attn=None):
    bsz, t_len, d = h.shape
    assert t_len % ts == 0
    row = lambda v: v.reshape(1, d)
    tile = pl.BlockSpec((1, ts, d), lambda b, t: (b, t, 0))
    args, specs = [h], [tile]
    if attn is not None:
        o, w_o, g_mix = attn
        args += [o, w_o, row(g_mix)]
        specs += [tile, _const_spec(w_o.shape), _const_spec((1, d))]
    args += [row(g2), row(g3), w_up, conv_w, w_down, tail0]
    specs += [_const_spec((1, d)), _const_spec((1, d)), _const_spec(w_up.shape), _const_spec(conv_w.shape),
              _const_spec(w_down.shape), _const_spec(tail0.shape)]
    out, tail = pl.pallas_call(
        functools.partial(_ffn_kernel, ts=ts, with_attn=attn is not None),
        grid=(bsz, t_len // ts),
        in_specs=specs,
        out_specs=[tile, pl.BlockSpec((1, SUBLANES, 2 * D_FF_PAD), lambda b, t: (b, 0, 0))],
        out_shape=[jax.ShapeDtypeStruct(h.shape, F32),
                   jax.ShapeDtypeStruct((bsz, SUBLANES, 2 * D_FF_PAD), F32)],
        scratch_shapes=[pltpu.VMEM((ts + SUBLANES, 2 * D_FF_PAD), F32), pltpu.VMEM((ts, D_FF_PAD), BF16)],
        compiler_params=_params(),
        name="conv_ffn",
    )(*args)
    return out, tail


def _fox_proj_kernel(h_ref, gq_ref, gkv_ref, wq_ref, wk_ref, wv_ref, wf2_ref, fgb_ref, sel_ref,
                     c0_ref, q_out, k_out, v_out, cend_ref, carry_s, *, ts):
    t = pl.program_id(1)

    @pl.when(t == 0)
    def _():
        carry_s[...] = c0_ref[...]

    h = h_ref[0]
    y = h * lax.rsqrt(jnp.mean(h * h, axis=-1, keepdims=True) + EPS)
    xq = (y * gq_ref[...]).astype(BF16)
    xkv = y * gkv_ref[...]
    xkv_hi = xkv.astype(BF16)
    xkv_lo = (xkv - xkv_hi.astype(F32)).astype(BF16)
    dot = functools.partial(jnp.dot, preferred_element_type=F32)
    zf2 = dot(xkv_hi, wf2_ref[...])
    zf = (zf2[:, :LANES] + zf2[:, LANES:] + dot(xkv_lo, wf2_ref[:, :LANES])
          + fgb_ref[...])
    ls = jnp.minimum(zf, 0.0) - jnp.log(1.0 + jnp.exp(-jnp.abs(zf)))
    c = carry_s[...] + _cumsum_rows(ls, _tril(ts).astype(BF16))
    carry_s[...] = c[ts - 1:ts, :]

    c2 = c * np.float32(LOG2E)
    c_hi = c2.astype(BF16).astype(F32)
    r = c2 - c_hi
    c_mid = r.astype(BF16).astype(F32)
    c_lo = (r - c_mid).astype(BF16).astype(F32)
    lane = lax.broadcasted_iota(jnp.int32, (ts, LANES), 1)
    parts = jnp.where(lane < B_HEADS, c_hi,
                      jnp.where(lane < 2 * B_HEADS, pltpu.roll(c_mid, B_HEADS, 1),
                                jnp.where(lane < 3 * B_HEADS, pltpu.roll(c_lo, 2 * B_HEADS, 1), 0.0)))
    parts = parts.astype(BF16)
    data = lane < B_HDIM
    plus = lane < B_HDIM + 3
    minus = lane < B_HDIM + 6
    heads_per_block = PROJ_COLS // B_HDIM
    for c0 in range(0, D_MODEL, PROJ_COLS):
        cols = slice(c0, c0 + PROJ_COLS)
        qb = dot(xq, wq_ref[:, cols]) * np.float32(LOG2E / np.sqrt(B_HDIM))
        kb = dot(xkv_hi, wk_ref[:, cols])
        vb = dot(xkv_hi, wv_ref[:, cols])
        hd0 = c0 // B_HDIM
        auxb = dot(parts, sel_ref[:, hd0 * ATT_PAD:(hd0 + heads_per_block) * ATT_PAD])
        for j in range(heads_per_block):
            def head(x):
                x = x[:, (j // 2) * LANES:(j // 2 + 1) * LANES]
                return pltpu.roll(x, B_HDIM, 1) if j % 2 else x
            a = auxb[:, j * ATT_PAD:(j + 1) * ATT_PAD]
            hd = hd0 + j
            q_out[0, hd] = jnp.where(data, head(qb), jnp.where(plus, a, jnp.where(minus, 1.0, 0.0))).astype(BF16)
            k_out[0, hd] = jnp.where(data, head(kb), jnp.where(plus, 1.0, jnp.where(minus, a, 0.0))).astype(BF16)
            v_out[0, hd] = jnp.where(data, head(vb), jnp.where(lane == B_HDIM, 1.0, 0.0)).astype(BF16)

    @pl.when(t == pl.num_programs(1) - 1)
    def _():
        cend_ref[0] = carry_s[...]


def _fox_sel_matrix():
    sel = np.zeros((LANES, B_HEADS * ATT_PAD), np.float32)
    for term in range(3):
        for hd in range(B_HEADS):
            sel[term * B_HEADS + hd, hd * ATT_PAD + B_HDIM + term] = 1.0
            sel[term * B_HEADS + hd, hd * ATT_PAD + B_HDIM + 3 + term] = -1.0
    return jnp.asarray(sel, BF16)


def _fox_proj(h, g_q, g_kv, w_q, w_k, w_v, w_f2, fg_b, c0, *, ts):
    bsz, t_len, d = h.shape
    assert t_len % ts == 0
    row = lambda v: v.reshape(1, d)
    sel = _fox_sel_matrix()
    head_shape = (bsz, B_HEADS, t_len, ATT_PAD)
    head_spec = pl.BlockSpec((1, B_HEADS, ts, ATT_PAD), lambda b, t: (b, 0, t, 0))
    return pl.pallas_call(
        functools.partial(_fox_proj_kernel, ts=ts),
        grid=(bsz, t_len // ts),
        in_specs=[
            pl.BlockSpec((1, ts, d), lambda b, t: (b, t, 0)),
            _const_spec((1, d)), _const_spec((1, d)),
            _const_spec(w_q.shape), _const_spec(w_k.shape), _const_spec(w_v.shape),
            _const_spec(w_f2.shape), _const_spec(fg_b.shape),
            _const_spec(sel.shape), _const_spec(c0.shape),
        ],
        out_specs=[head_spec, head_spec, head_spec,
                   pl.BlockSpec((1, 1, LANES), lambda b, t: (b, 0, 0))],
        out_shape=[jax.ShapeDtypeStruct(head_shape, BF16)] * 3 + [jax.ShapeDtypeStruct((bsz, 1, LANES), F32)],
        scratch_shapes=[pltpu.VMEM((1, LANES), F32)],
        compiler_params=_params(),
        name="fox_proj",
    )(h, row(g_q), row(g_kv), w_q, w_k, w_v, w_f2, fg_b, sel, c0)


def _fox_attn_kernel(*refs, tq, hp, n_meta):
    if n_meta:
        q_ref, k_ref, v_ref, km_ref, vm_ref, o_ref, kcat_s, vcat_s, s_a, s_b, p_a, p_b = refs
    else:
        q_ref, k_ref, v_ref, o_ref, kcat_s, vcat_s, s_a, s_b, p_a, p_b = refs
    t_len = q_ref.shape[2]
    n_keys = n_meta + t_len
    w_max = kcat_s.shape[1]
    for hh in range(hp):
        if n_meta:
            kcat_s[hh, 0:n_meta] = km_ref[0, hh]
            vcat_s[hh, 0:n_meta] = vm_ref[0, hh]
        kcat_s[hh, n_meta:n_keys] = k_ref[0, hh]
        vcat_s[hh, n_meta:n_keys] = v_ref[0, hh]
        if w_max > n_keys:
            vcat_s[hh, n_keys:w_max] = jnp.zeros((w_max - n_keys, vcat_s.shape[2]), BF16)

    units = [(i, hh) for i in range(-(-n_keys // tq)) for hh in range(hp)]

    def geometry(i):
        k0 = i * tq
        a, e = max(n_meta, k0), min(k0 + tq, n_keys)
        return k0, a, e, e - a, -(-e // LANES) * LANES

    def logit_pieces(u):
        (i, hh), s_s = units[u], (s_a, s_b)[u % 2]
        k0, a, e, nr, w = geometry(i)
        diag_ok = (lax.broadcasted_iota(jnp.int32, (nr, e - k0), 1)
                   <= lax.broadcasted_iota(jnp.int32, (nr, e - k0), 0) + (a - k0))

        def piece(kt):
            def run():
                q = q_ref[0, hh, a - n_meta:e - n_meta, :]
                if kt < k0:
                    s_s[0:nr, kt:kt + tq] = _dot_nt(q, kcat_s[hh, kt:kt + tq])
                else:
                    s_s[0:nr, k0:e] = jnp.where(diag_ok, _dot_nt(q, kcat_s[hh, k0:e]), MASKED)
                    if w > e:
                        s_s[0:nr, e:w] = jnp.full((nr, w - e), MASKED, F32)
            return run
        return [piece(kt) for kt in range(0, e, tq)]

    def softmax_pieces(u):
        (i, hh), s_s, p_s = units[u], (s_a, s_b)[u % 2], (p_a, p_b)[u % 2]
        k0, a, e, nr, w = geometry(i)
        rs = min(nr, ATT_STRIP)

        def piece(r):
            def run():
                x = s_s[r:r + rs, 0:w]
                m = jnp.max(x, axis=-1, keepdims=True)
                p_s[r:r + rs, 0:w] = jnp.exp2(x - m).astype(BF16)
            return run
        return [piece(r) for r in range(0, nr, rs)]

    def value_pieces(u):
        (i, hh), p_s = units[u], (p_a, p_b)[u % 2]
        k0, a, e, nr, w = geometry(i)

        def run():
            acc = jnp.dot(p_s[0:nr, 0:w], vcat_s[hh, 0:w], preferred_element_type=F32)
            o = acc[:, :B_HDIM] / acc[:, B_HDIM:B_HDIM + 1]
            o_ref[0, a - n_meta:e - n_meta, hh * B_HDIM:(hh + 1) * B_HDIM] = o.astype(BF16)
        return [run]

    for step in range(len(units) + 2):
        streams = [f(u) for f, u in ((value_pieces, step - 2), (softmax_pieces, step - 1), (logit_pieces, step))
                   if 0 <= u < len(units)]
        keyed = [((j + 0.5) / len(st), n, run) for n, st in enumerate(streams) for j, run in enumerate(st)]
        for _, _, run in sorted(keyed, key=lambda t: t[:2]):
            run()


def _fox_attn(q, k, v, meta_kv, *, tq, hp=HEADS_PER_STEP):
    bsz, heads, t_len, pad = q.shape
    assert heads % hp == 0
    head_spec = pl.BlockSpec((1, hp, t_len, pad), lambda b, g: (b, g, 0, 0))
    args, specs = [q, k, v], [head_spec] * 3
    n_meta = 0
    if meta_kv is not None:
        n_meta = meta_kv[0].shape[2]
        args += list(meta_kv)
        specs += [pl.BlockSpec((1, hp, n_meta, pad), lambda b, g: (0, g, 0, 0))] * 2
    w_max = -(-(n_meta + t_len) // LANES) * LANES
    return pl.pallas_call(
        functools.partial(_fox_attn_kernel, tq=tq, hp=hp, n_meta=n_meta),
        grid=(bsz, heads // hp),
        in_specs=specs,
        out_specs=pl.BlockSpec((1, t_len, hp * B_HDIM), lambda b, g: (b, 0, g)),
        out_shape=jax.ShapeDtypeStruct((bsz, t_len, heads * B_HDIM), BF16),
        scratch_shapes=[pltpu.VMEM((hp, w_max, pad), BF16)] * 2
                       + [pltpu.VMEM((tq, w_max), F32)] * 2 + [pltpu.VMEM((tq, w_max), BF16)] * 2,
        compiler_params=_params(),
        name="fox_attention",
    )(*args)


def _pad_cols(w, n):
    return jnp.pad(w, ((0, 0), (0, n - w.shape[1])))


def _prep_ffn(w_up, conv_w, w_down):
    up = jnp.concatenate([_pad_cols(w_up[:, :D_FF], D_FF_PAD), _pad_cols(w_up[:, D_FF:], D_FF_PAD)], axis=1)
    cw = jnp.concatenate([_pad_cols(conv_w[:, :D_FF], D_FF_PAD), _pad_cols(conv_w[:, D_FF:], D_FF_PAD)], axis=1)
    down = jnp.pad(w_down, ((0, D_FF_PAD - D_FF), (0, 0)))
    return up.astype(BF16), cw.astype(F32), down.astype(BF16)


def _trunk(h, params, carry_in, *, ts, ts_ffn, chunk, tq):
    s0, tail_a, tail_b, c0, meta_kv = carry_in
    p = params
    h1, s_fin = _hgrn2_layer(h, p["g"][0, 0], p["g"][0, 1], p["lb_logits"], p["head_gain"], p["w_in"],
                             p["a_w_out"], s0, ts=ts, chunk=chunk, layer=0)
    h2, tail_a_out = _ffn_layer(h1, p["g"][0, 2], p["g"][0, 3], *p["ffn"][0], tail_a, ts=ts_ffn)
    q, k, v, c_end = _fox_proj(h2, p["g"][1, 0], p["kv_norm"], p["w_q"], p["w_k"], p["w_v"],
                               p["w_f2"], p["fg_b"], c0, ts=ts)
    o = _fox_attn(q, k, v, meta_kv, tq=tq)
    h4, tail_b_out = _ffn_layer(h2, p["g"][1, 2], p["g"][1, 3], *p["ffn"][1], tail_b, ts=ts_ffn,
                                attn=(o, p["b_w_out"], p["g"][1, 1]))
    return h4, (s_fin[0], tail_a_out[0], tail_b_out[0], c_end[0], (k[:1], v[:1]))


def kernel(x, meta_tokens, norm_gains, a_w_in, a_lb_logits, a_head_norm, a_w_out, kv_norm, kv_w, fg_b,
           b_w_q, b_w_out, ffn_w_up, ffn_conv, ffn_w_down):
    d = D_MODEL
    assert x.shape[2] == d and x.shape[1] % SEQ_TILE == 0 and meta_tokens.shape == (N_META, d)
    assert norm_gains.shape[0] == 2 and a_w_in.shape[0] == 1 and b_w_q.shape[0] == 1

    w_f = jnp.pad(kv_w[:, 2 * d:], ((0, 0), (0, LANES - B_HEADS))).astype(F32)
    w_f_hi = w_f.astype(BF16)
    params = {
        "g": norm_gains.astype(F32),
        "lb_logits": a_lb_logits.astype(F32),
        "head_gain": a_head_norm[0].astype(F32),
        "w_in": a_w_in[0].astype(BF16),
        "a_w_out": a_w_out[0].astype(BF16),
        "kv_norm": kv_norm.astype(F32),
        "w_q": b_w_q[0].astype(BF16),
        "w_k": kv_w[:, :d].astype(BF16),
        "w_v": kv_w[:, d:2 * d].astype(BF16),
        "w_f2": jnp.concatenate([w_f_hi, (w_f - w_f_hi.astype(F32)).astype(BF16)], axis=1),
        "fg_b": jnp.pad(fg_b.astype(F32), (0, LANES - B_HEADS)).reshape(1, LANES),
        "b_w_out": b_w_out[0].astype(BF16),
        "ffn": [_prep_ffn(ffn_w_up[l], ffn_conv[l], ffn_w_down[l]) for l in range(2)],
    }

    zero_carry = (jnp.zeros((A_HEADS, A_DK, A_DK), F32),
                  jnp.zeros((SUBLANES, 2 * D_FF_PAD), F32), jnp.zeros((SUBLANES, 2 * D_FF_PAD), F32),
                  jnp.zeros((1, LANES), F32), None)
    _, meta_carry = _trunk(meta_tokens[None].astype(F32), params, zero_carry,
                           ts=N_META, ts_ffn=N_META, chunk=N_META, tq=N_META)
    out, _ = _trunk(x, params, meta_carry, ts=SEQ_TILE, ts_ffn=FFN_TILE, chunk=A_CHUNK, tq=Q_TILE)
    return out
```

```python
import functools

import numpy as np
import jax
import jax.numpy as jnp
from jax import lax
from jax.experimental import pallas as pl
from jax.experimental.pallas import tpu as pltpu

F32 = jnp.float32
BF16 = jnp.bfloat16

D_MODEL = 1024
N_META = 16
A_HEADS = 8
A_DK = D_MODEL // A_HEADS
A_CHUNK = 64
B_HEADS = 16
B_HDIM = D_MODEL // B_HEADS
D_FF = 2752
EPS = 1e-6

LANES = 128
SUBLANES = 8
MXU_DIM = 256
D_FF_PAD = -(-D_FF // MXU_DIM) * MXU_DIM
ATT_PAD = LANES
VMEM_LIMIT = 56 * 1024 * 1024

SEQ_TILE = 512
GLA_TILE = 256
FFN_TILE = 1024
Q_TILE = 256
PROJ_COLS = MXU_DIM
GATE_COLS = MXU_DIM
CONV_ROWS = 64
CONV_COLS = 256
HEADS_PER_STEP = 4
ATT_STRIP = 16
MASKED = -1e30
LOG2E = 1.4426950408889634


def _rms(x, g):
    return x * lax.rsqrt(jnp.mean(x * x, axis=-1, keepdims=True) + EPS) * g


def _sigmoid(x):
    return 1.0 / (1.0 + jnp.exp(-x))


def _split3(x):
    hi = x.astype(BF16)
    r = x - hi.astype(F32)
    mid = r.astype(BF16)
    lo = (r - mid.astype(F32)).astype(BF16)
    return hi, mid, lo


def _tril(n):
    row = lax.broadcasted_iota(jnp.int32, (n, n), 0)
    col = lax.broadcasted_iota(jnp.int32, (n, n), 1)
    return row >= col


def _cumsum_rows(x, tril_bf16):
    hi, mid, lo = _split3(x)
    dot = functools.partial(jnp.dot, preferred_element_type=F32)
    return dot(tril_bf16, hi) + dot(tril_bf16, mid) + dot(tril_bf16, lo)


def _dot_nt(a, b):
    return lax.dot_general(a, b, (((1,), (1,)), ((), ())), preferred_element_type=F32)


def _const_spec(shape):
    nd = len(shape)
    return pl.BlockSpec(shape, lambda *_: (0,) * nd, pipeline_mode=pl.Buffered(1))


def _params():
    return pltpu.CompilerParams(dimension_semantics=("arbitrary", "arbitrary"),
                                vmem_limit_bytes=VMEM_LIMIT)


def _hgrn2_kernel(x_ref, gpre_ref, gpost_ref, lbl_ref, hg_ref, win_ref, wout_ref, s0_ref,
                  out_ref, sfin_ref, proj_s, qin_s, kin_s, kout_s, v_s, dec_s, y_s, st_s, *, ts, chunk, layer):
    t = pl.program_id(1)
    n_chunks = ts // chunk

    @pl.when(t == 0)
    def _():
        st_s[...] = s0_ref[...]

    lbl = lbl_ref[...]
    e = jnp.exp(lbl - jnp.max(lbl, axis=0, keepdims=True))
    lb = jnp.sum(e[:layer + 1], axis=0, keepdims=True) / jnp.sum(e, axis=0, keepdims=True)

    xn = _rms(x_ref[0], gpre_ref[...]).astype(BF16)
    proj_s[...] = jnp.dot(xn, win_ref[...], preferred_element_type=F32)

    tril3 = jnp.concatenate([_tril(chunk).astype(BF16)] * 3, axis=1)
    for c in range(n_chunks):
        rows = slice(c * chunk, (c + 1) * chunk)
        for c0 in range(0, D_MODEL, GATE_COLS):
            cols = slice(c0, c0 + GATE_COLS)
            lbc = lb[:, cols]
            f = lbc + (1.0 - lbc) * _sigmoid(proj_s[rows, D_MODEL + c0:D_MODEL + c0 + GATE_COLS])
            k = 1.0 - f
            b = jnp.dot(tril3, jnp.concatenate(_split3(jnp.log(f)), axis=0), preferred_element_type=F32)
            dec = jnp.exp(b[chunk - 1:chunk, :])
            k_in = k * jnp.exp(-b)
            qin_s[rows, cols] = (proj_s[rows, cols] * jnp.exp(b)).astype(BF16)
            kin_s[rows, cols] = k_in.astype(BF16)
            kout_s[rows, cols] = (k_in * dec).astype(BF16)
            v_s[rows, cols] = proj_s[rows, 2 * D_MODEL + c0:2 * D_MODEL + c0 + GATE_COLS].astype(BF16)
            dec_s[c:c + 1, cols] = dec

    gt = min(ts, GLA_TILE)
    sub_chunks = gt // chunk
    row_c = lax.broadcasted_iota(jnp.int32, (gt, gt), 0)
    col_c = lax.broadcasted_iota(jnp.int32, (gt, gt), 1)
    intra = (row_c >= col_c) & (row_c // chunk == col_c // chunk)
    vt_chunk = lax.broadcasted_iota(jnp.int32, (A_DK, gt), 1) // chunk
    hgain = hg_ref[...]
    for r0 in range(0, ts, gt):
        rows = slice(r0, r0 + gt)
        for h in range(A_HEADS):
            cols = slice(h * A_DK, (h + 1) * A_DK)
            q_in = qin_s[rows, cols]
            attn = jnp.where(intra, _dot_nt(q_in, kin_s[rows, cols]), 0.0).astype(BF16)
            o_intra = jnp.dot(attn, v_s[rows, cols], preferred_element_type=F32)
            v_t = proj_s[rows, 2 * D_MODEL + h * A_DK:2 * D_MODEL + (h + 1) * A_DK].T.astype(BF16)
            v_t_blocks = jnp.concatenate([jnp.where(vt_chunk == c, v_t, jnp.zeros_like(v_t))
                                          for c in range(sub_chunks)], axis=0)
            d_st = jnp.dot(v_t_blocks, kout_s[rows, cols], preferred_element_type=F32)
            st = st_s[h]
            o_inter = []
            for c in range(sub_chunks):
                o_inter.append(_dot_nt(q_in[c * chunk:(c + 1) * chunk], st.astype(BF16)))
                dec = dec_s[r0 // chunk + c:r0 // chunk + c + 1, cols]
                st = st * dec + d_st[c * A_DK:(c + 1) * A_DK]
            st_s[h] = st
            o = o_intra + jnp.concatenate(o_inter, axis=0)
            o = o * lax.rsqrt(jnp.mean(o * o, axis=-1, keepdims=True) + EPS)
            gg = proj_s[rows, 3 * D_MODEL + h * A_DK:3 * D_MODEL + (h + 1) * A_DK]
            y_s[rows, cols] = (o * hgain[:, cols] * (gg * _sigmoid(gg))).astype(BF16)

    mix = jnp.dot(y_s[...], wout_ref[...], preferred_element_type=F32)
    out_ref[0] = x_ref[0] + _rms(mix, gpost_ref[...])

    @pl.when(t == pl.num_programs(1) - 1)
    def _():
        sfin_ref[0] = st_s[...]


def _hgrn2_layer(h, g_pre, g_post, lb_logits, head_gain, w_in, w_out, s0, *, ts, chunk, layer):
    bsz, t_len, d = h.shape
    assert t_len % ts == 0 and ts % chunk == 0
    row = lambda v: v.reshape(1, d)
    st_shape = (A_HEADS, A_DK, A_DK)
    out, s_fin = pl.pallas_call(
        functools.partial(_hgrn2_kernel, ts=ts, chunk=chunk, layer=layer),
        grid=(bsz, t_len // ts),
        in_specs=[
            pl.BlockSpec((1, ts, d), lambda b, t: (b, t, 0)),
            _const_spec((1, d)), _const_spec((1, d)), _const_spec(lb_logits.shape), _const_spec((1, d)),
            _const_spec(w_in.shape), _const_spec(w_out.shape), _const_spec(st_shape),
        ],
        out_specs=[
            pl.BlockSpec((1, ts, d), lambda b, t: (b, t, 0)),
            pl.BlockSpec((1,) + st_shape, lambda b, t: (b, 0, 0, 0)),
        ],
        out_shape=[jax.ShapeDtypeStruct(h.shape, F32),
                   jax.ShapeDtypeStruct((bsz,) + st_shape, F32)],
        scratch_shapes=[pltpu.VMEM((ts, 4 * d), F32)] + [pltpu.VMEM((ts, d), BF16)] * 4
                       + [pltpu.VMEM((max(ts // chunk, SUBLANES), d), F32), pltpu.VMEM((ts, d), BF16),
                          pltpu.VMEM(st_shape, F32)],
        compiler_params=_params(),
        name="hgrn2_mixer",
    )(h, row(g_pre), row(g_post), lb_logits, row(head_gain), w_in, w_out, s0)
    return out, s_fin


def _ffn_kernel(*refs, ts, with_attn):
    if with_attn:
        (h_ref, o_ref, wo_ref, gmix_ref, g2_ref, g3_ref, wup_ref, conv_ref, wdown_ref, tail0_ref,
         out_ref, tail_ref, u_a, u_b, tail_s, act_s) = refs
    else:
        (h_ref, g2_ref, g3_ref, wup_ref, conv_ref, wdown_ref, tail0_ref,
         out_ref, tail_ref, u_a, u_b, tail_s, act_s) = refs
    t = pl.program_id(1)
    halo = SUBLANES

    @pl.when(t == 0)
    def _():
        tail_s[...] = tail0_ref[...]

    h = h_ref[0]
    if with_attn:
        mix = jnp.dot(o_ref[0], wo_ref[...], preferred_element_type=F32)
        h = h + _rms(mix, gmix_ref[...])
    xn = _rms(h, g2_ref[...]).astype(BF16)
    rb = min(CONV_ROWS, ts)
    for j in range(D_FF_PAD // CONV_COLS):
        u_s = (u_a, u_b)[j % 2]
        for half, c0 in enumerate((j * CONV_COLS, D_FF_PAD + j * CONV_COLS)):
            here = slice(half * CONV_COLS, (half + 1) * CONV_COLS)
            u_s[0:halo, here] = tail_s[:, c0:c0 + CONV_COLS]
            u_s[halo:halo + ts, here] = jnp.dot(xn, wup_ref[:, c0:c0 + CONV_COLS], preferred_element_type=F32)
            tail_s[:, c0:c0 + CONV_COLS] = u_s[ts:ts + halo, here]
        for r0 in range(0, ts, rb):
            def conv(half, c0):
                w = conv_ref[:, c0:c0 + CONV_COLS]
                blk = u_s[r0:r0 + rb + halo, half * CONV_COLS:(half + 1) * CONV_COLS]
                tap = lambda d: (pltpu.roll(blk, d, 0) if d else blk)[halo:halo + rb]
                return w[0:1] * tap(2) + w[1:2] * tap(1) + w[2:3] * tap(0)
            cg, cv = conv(0, j * CONV_COLS), conv(1, D_FF_PAD + j * CONV_COLS)
            act_s[r0:r0 + rb, j * CONV_COLS:(j + 1) * CONV_COLS] = (cg * _sigmoid(cg) * cv).astype(BF16)

    ff = jnp.dot(act_s[...], wdown_ref[...], preferred_element_type=F32)
    out_ref[0] = h + _rms(ff, g3_ref[...])

    @pl.when(t == pl.num_programs(1) - 1)
    def _():
        tail_ref[0] = tail_s[...]


def _ffn_layer(h, g2, g3, w_up, conv_w, w_down, tail0, *, ts, attn=None):
    bsz, t_len, d = h.shape
    assert t_len % ts == 0
    row = lambda v: v.reshape(1, d)
    tile = pl.BlockSpec((1, ts, d), lambda b, t: (b, t, 0))
    args, specs = [h], [tile]
    if attn is not None:
        o, w_o, g_mix = attn
        args += [o, w_o, row(g_mix)]
        specs += [tile, _const_spec(w_o.shape), _const_spec((1, d))]
    args += [row(g2), row(g3), w_up, conv_w, w_down, tail0]
    specs += [_const_spec((1, d)), _const_spec((1, d)), _const_spec(w_up.shape), _const_spec(conv_w.shape),
              _const_spec(w_down.shape), _const_spec(tail0.shape)]
    out, tail = pl.pallas_call(
        functools.partial(_ffn_kernel, ts=ts, with_attn=attn is not None),
        grid=(bsz, t_len // ts),
        in_specs=specs,
        out_specs=[tile, pl.BlockSpec((1, SUBLANES, 2 * D_FF_PAD), lambda b, t: (b, 0, 0))],
        out_shape=[jax.ShapeDtypeStruct(h.shape, F32),
                   jax.ShapeDtypeStruct((bsz, SUBLANES, 2 * D_FF_PAD), F32)],
        scratch_shapes=[pltpu.VMEM((ts + SUBLANES, 2 * CONV_COLS), F32)] * 2
                       + [pltpu.VMEM((SUBLANES, 2 * D_FF_PAD), F32), pltpu.VMEM((ts, D_FF_PAD), BF16)],
        compiler_params=_params(),
        name="conv_ffn",
    )(*args)
    return out, tail


def _fox_proj_kernel(h_ref, gq_ref, gkv_ref, wq_ref, wk_ref, wv_ref, wf2_ref, fgb_ref, sel_ref,
                     c0_ref, q_out, k_out, v_out, cend_ref, carry_s, *, ts):
    t = pl.program_id(1)

    @pl.when(t == 0)
    def _():
        carry_s[...] = c0_ref[...]

    h = h_ref[0]
    y = h * lax.rsqrt(jnp.mean(h * h, axis=-1, keepdims=True) + EPS)
    xq = (y * gq_ref[...]).astype(BF16)
    xkv = y * gkv_ref[...]
    xkv_hi = xkv.astype(BF16)
    xkv_lo = (xkv - xkv_hi.astype(F32)).astype(BF16)
    dot = functools.partial(jnp.dot, preferred_element_type=F32)
    zf2 = dot(xkv_hi, wf2_ref[...])
    zf = (zf2[:, :LANES] + zf2[:, LANES:] + dot(xkv_lo, wf2_ref[:, :LANES])
          + fgb_ref[...])
    ls = jnp.minimum(zf, 0.0) - jnp.log(1.0 + jnp.exp(-jnp.abs(zf)))
    c = carry_s[...] + _cumsum_rows(ls, _tril(ts).astype(BF16))
    carry_s[...] = c[ts - 1:ts, :]

    c2 = c * np.float32(LOG2E)
    c_hi = c2.astype(BF16).astype(F32)
    r = c2 - c_hi
    c_mid = r.astype(BF16).astype(F32)
    c_lo = (r - c_mid).astype(BF16).astype(F32)
    lane = lax.broadcasted_iota(jnp.int32, (ts, LANES), 1)
    parts = jnp.where(lane < B_HEADS, c_hi,
                      jnp.where(lane < 2 * B_HEADS, pltpu.roll(c_mid, B_HEADS, 1),
                                jnp.where(lane < 3 * B_HEADS, pltpu.roll(c_lo, 2 * B_HEADS, 1), 0.0)))
    parts = parts.astype(BF16)
    data = lane < B_HDIM
    plus = lane < B_HDIM + 3
    minus = lane < B_HDIM + 6
    heads_per_block = PROJ_COLS // B_HDIM
    for c0 in range(0, D_MODEL, PROJ_COLS):
        cols = slice(c0, c0 + PROJ_COLS)
        qb = dot(xq, wq_ref[:, cols]) * np.float32(LOG2E / np.sqrt(B_HDIM))
        kb = dot(xkv_hi, wk_ref[:, cols])
        vb = dot(xkv_hi, wv_ref[:, cols])
        hd0 = c0 // B_HDIM
        auxb = dot(parts, sel_ref[:, hd0 * ATT_PAD:(hd0 + heads_per_block) * ATT_PAD])
        for j in range(heads_per_block):
            def head(x):
                x = x[:, (j // 2) * LANES:(j // 2 + 1) * LANES]
                return pltpu.roll(x, B_HDIM, 1) if j % 2 else x
            a = auxb[:, j * ATT_PAD:(j + 1) * ATT_PAD]
            hd = hd0 + j
            q_out[0, hd] = jnp.where(data, head(qb), jnp.where(plus, a, jnp.where(minus, 1.0, 0.0))).astype(BF16)
            k_out[0, hd] = jnp.where(data, head(kb), jnp.where(plus, 1.0, jnp.where(minus, a, 0.0))).astype(BF16)
            v_out[0, hd] = jnp.where(data, head(vb), jnp.where(lane == B_HDIM, 1.0, 0.0)).astype(BF16)

    @pl.when(t == pl.num_programs(1) - 1)
    def _():
        cend_ref[0] = carry_s[...]


def _fox_sel_matrix():
    sel = np.zeros((LANES, B_HEADS * ATT_PAD), np.float32)
    for term in range(3):
        for hd in range(B_HEADS):
            sel[term * B_HEADS + hd, hd * ATT_PAD + B_HDIM + term] = 1.0
            sel[term * B_HEADS + hd, hd * ATT_PAD + B_HDIM + 3 + term] = -1.0
    return jnp.asarray(sel, BF16)


def _fox_proj(h, g_q, g_kv, w_q, w_k, w_v, w_f2, fg_b, c0, *, ts):
    bsz, t_len, d = h.shape
    assert t_len % ts == 0
    row = lambda v: v.reshape(1, d)
    sel = _fox_sel_matrix()
    head_shape = (bsz, B_HEADS, t_len, ATT_PAD)
    head_spec = pl.BlockSpec((1, B_HEADS, ts, ATT_PAD), lambda b, t: (b, 0, t, 0))
    return pl.pallas_call(
        functools.partial(_fox_proj_kernel, ts=ts),
        grid=(bsz, t_len // ts),
        in_specs=[
            pl.BlockSpec((1, ts, d), lambda b, t: (b, t, 0)),
            _const_spec((1, d)), _const_spec((1, d)),
            _const_spec(w_q.shape), _const_spec(w_k.shape), _const_spec(w_v.shape),
            _const_spec(w_f2.shape), _const_spec(fg_b.shape),
            _const_spec(sel.shape), _const_spec(c0.shape),
        ],
        out_specs=[head_spec, head_spec, head_spec,
                   pl.BlockSpec((1, 1, LANES), lambda b, t: (b, 0, 0))],
        out_shape=[jax.ShapeDtypeStruct(head_shape, BF16)] * 3 + [jax.ShapeDtypeStruct((bsz, 1, LANES), F32)],
        scratch_shapes=[pltpu.VMEM((1, LANES), F32)],
        compiler_params=_params(),
        name="fox_proj",
    )(h, row(g_q), row(g_kv), w_q, w_k, w_v, w_f2, fg_b, sel, c0)


def _fox_attn_kernel(*refs, tq, hp, n_meta):
    if n_meta:
        q_ref, k_ref, v_ref, km_ref, vm_ref, o_ref, kcat_s, vcat_s, s_a, s_b, p_a, p_b = refs
    else:
        q_ref, k_ref, v_ref, o_ref, kcat_s, vcat_s, s_a, s_b, p_a, p_b = refs
    t_len = q_ref.shape[2]
    n_keys = n_meta + t_len
    w_max = kcat_s.shape[1]
    for hh in range(hp):
        if n_meta:
            kcat_s[hh, 0:n_meta] = km_ref[0, hh]
            vcat_s[hh, 0:n_meta] = vm_ref[0, hh]
        kcat_s[hh, n_meta:n_keys] = k_ref[0, hh]
        vcat_s[hh, n_meta:n_keys] = v_ref[0, hh]
        if w_max > n_keys:
            vcat_s[hh, n_keys:w_max] = jnp.zeros((w_max - n_keys, vcat_s.shape[2]), BF16)

    units = [(i, hh) for i in range(-(-n_keys // tq)) for hh in range(hp)]

    def geometry(i):
        k0 = i * tq
        a, e = max(n_meta, k0), min(k0 + tq, n_keys)
        return k0, a, e, e - a, -(-e // LANES) * LANES

    def logit_pieces(u):
        (i, hh), s_s = units[u], (s_a, s_b)[u % 2]
        k0, a, e, nr, w = geometry(i)
        diag_ok = (lax.broadcasted_iota(jnp.int32, (nr, e - k0), 1)
                   <= lax.broadcasted_iota(jnp.int32, (nr, e - k0), 0) + (a - k0))

        def piece(kt):
            def run():
                q = q_ref[0, hh, a - n_meta:e - n_meta, :]
                if kt < k0:
                    s_s[0:nr, kt:kt + tq] = _dot_nt(q, kcat_s[hh, kt:kt + tq])
                else:
                    s_s[0:nr, k0:e] = jnp.where(diag_ok, _dot_nt(q, kcat_s[hh, k0:e]), MASKED)
                    if w > e:
                        s_s[0:nr, e:w] = jnp.full((nr, w - e), MASKED, F32)
            return run
        return [piece(kt) for kt in range(0, e, tq)]

    def softmax_pieces(u):
        (i, hh), s_s, p_s = units[u], (s_a, s_b)[u % 2], (p_a, p_b)[u % 2]
        k0, a, e, nr, w = geometry(i)
        rs = min(nr, ATT_STRIP)

        def piece(r):
            def run():
                x = s_s[r:r + rs, 0:w]
                m = jnp.max(x, axis=-1, keepdims=True)
                p_s[r:r + rs, 0:w] = jnp.exp2(x - m).astype(BF16)
            return run
        return [piece(r) for r in range(0, nr, rs)]

    def value_pieces(u):
        (i, hh), p_s = units[u], (p_a, p_b)[u % 2]
        k0, a, e, nr, w = geometry(i)

        def run():
            acc = jnp.dot(p_s[0:nr, 0:w], vcat_s[hh, 0:w], preferred_element_type=F32)
            o = acc[:, :B_HDIM] / acc[:, B_HDIM:B_HDIM + 1]
            o_ref[0, a - n_meta:e - n_meta, hh * B_HDIM:(hh + 1) * B_HDIM] = o.astype(BF16)
        return [run]

    for step in range(len(units) + 2):
        streams = [f(u) for f, u in ((value_pieces, step - 2), (softmax_pieces, step - 1), (logit_pieces, step))
                   if 0 <= u < len(units)]
        keyed = [((j + 0.5) / len(st), n, run) for n, st in enumerate(streams) for j, run in enumerate(st)]
        for _, _, run in sorted(keyed, key=lambda t: t[:2]):
            run()


def _fox_attn(q, k, v, meta_kv, *, tq, hp=HEADS_PER_STEP):
    bsz, heads, t_len, pad = q.shape
    assert heads % hp == 0
    head_spec = pl.BlockSpec((1, hp, t_len, pad), lambda b, g: (b, g, 0, 0))
    args, specs = [q, k, v], [head_spec] * 3
    n_meta = 0
    if meta_kv is not None:
        n_meta = meta_kv[0].shape[2]
        args += list(meta_kv)
        specs += [pl.BlockSpec((1, hp, n_meta, pad), lambda b, g: (0, g, 0, 0))] * 2
    w_max = -(-(n_meta + t_len) // LANES) * LANES
    return pl.pallas_call(
        functools.partial(_fox_attn_kernel, tq=tq, hp=hp, n_meta=n_meta),
        grid=(bsz, heads // hp),
        in_specs=specs,
        out_specs=pl.BlockSpec((1, t_len, hp * B_HDIM), lambda b, g: (b, 0, g)),
        out_shape=jax.ShapeDtypeStruct((bsz, t_len, heads * B_HDIM), BF16),
        scratch_shapes=[pltpu.VMEM((hp, w_max, pad), BF16)] * 2
                       + [pltpu.VMEM((tq, w_max), F32)] * 2 + [pltpu.VMEM((tq, w_max), BF16)] * 2,
        compiler_params=_params(),
        name="fox_attention",
    )(*args)


def _pad_cols(w, n):
    return jnp.pad(w, ((0, 0), (0, n - w.shape[1])))


def _prep_ffn(w_up, conv_w, w_down):
    up = jnp.concatenate([_pad_cols(w_up[:, :D_FF], D_FF_PAD), _pad_cols(w_up[:, D_FF:], D_FF_PAD)], axis=1)
    cw = jnp.concatenate([_pad_cols(conv_w[:, :D_FF], D_FF_PAD), _pad_cols(conv_w[:, D_FF:], D_FF_PAD)], axis=1)
    down = jnp.pad(w_down, ((0, D_FF_PAD - D_FF), (0, 0)))
    return up.astype(BF16), cw.astype(F32), down.astype(BF16)


def _trunk(h, params, carry_in, *, ts, ts_ffn, chunk, tq):
    s0, tail_a, tail_b, c0, meta_kv = carry_in
    p = params
    h1, s_fin = _hgrn2_layer(h, p["g"][0, 0], p["g"][0, 1], p["lb_logits"], p["head_gain"], p["w_in"],
                             p["a_w_out"], s0, ts=ts, chunk=chunk, layer=0)
    h2, tail_a_out = _ffn_layer(h1, p["g"][0, 2], p["g"][0, 3], *p["ffn"][0], tail_a, ts=ts_ffn)
    q, k, v, c_end = _fox_proj(h2, p["g"][1, 0], p["kv_norm"], p["w_q"], p["w_k"], p["w_v"],
                               p["w_f2"], p["fg_b"], c0, ts=ts)
    o = _fox_attn(q, k, v, meta_kv, tq=tq)
    h4, tail_b_out = _ffn_layer(h2, p["g"][1, 2], p["g"][1, 3], *p["ffn"][1], tail_b, ts=ts_ffn,
                                attn=(o, p["b_w_out"], p["g"][1, 1]))
    return h4, (s_fin[0], tail_a_out[0], tail_b_out[0], c_end[0], (k[:1], v[:1]))


def kernel(x, meta_tokens, norm_gains, a_w_in, a_lb_logits, a_head_norm, a_w_out, kv_norm, kv_w, fg_b,
           b_w_q, b_w_out, ffn_w_up, ffn_conv, ffn_w_down):
    d = D_MODEL
    assert x.shape[2] == d and x.shape[1] % SEQ_TILE == 0 and meta_tokens.shape == (N_META, d)
    assert norm_gains.shape[0] == 2 and a_w_in.shape[0] == 1 and b_w_q.shape[0] == 1

    w_f = jnp.pad(kv_w[:, 2 * d:], ((0, 0), (0, LANES - B_HEADS))).astype(F32)
    w_f_hi = w_f.astype(BF16)
    params = {
        "g": norm_gains.astype(F32),
        "lb_logits": a_lb_logits.astype(F32),
        "head_gain": a_head_norm[0].astype(F32),
        "w_in": a_w_in[0].astype(BF16),
        "a_w_out": a_w_out[0].astype(BF16),
        "kv_norm": kv_norm.astype(F32),
        "w_q": b_w_q[0].astype(BF16),
        "w_k": kv_w[:, :d].astype(BF16),
        "w_v": kv_w[:, d:2 * d].astype(BF16),
        "w_f2": jnp.concatenate([w_f_hi, (w_f - w_f_hi.astype(F32)).astype(BF16)], axis=1),
        "fg_b": jnp.pad(fg_b.astype(F32), (0, LANES - B_HEADS)).reshape(1, LANES),
        "b_w_out": b_w_out[0].astype(BF16),
        "ffn": [_prep_ffn(ffn_w_up[l], ffn_conv[l], ffn_w_down[l]) for l in range(2)],
    }

    zero_carry = (jnp.zeros((A_HEADS, A_DK, A_DK), F32),
                  jnp.zeros((SUBLANES, 2 * D_FF_PAD), F32), jnp.zeros((SUBLANES, 2 * D_FF_PAD), F32),
                  jnp.zeros((1, LANES), F32), None)
    _, meta_carry = _trunk(meta_tokens[None].astype(F32), params, zero_carry,
                           ts=N_META, ts_ffn=N_META, chunk=N_META, tq=N_META)
    out, _ = _trunk(x, params, meta_carry, ts=SEQ_TILE, ts_ffn=FFN_TILE, chunk=A_CHUNK, tq=Q_TILE)
    return out
```

```python
import functools

import numpy as np
import jax
import jax.numpy as jnp
from jax import lax
from jax.experimental import pallas as pl
from jax.experimental.pallas import tpu as pltpu

F32 = jnp.float32
BF16 = jnp.bfloat16

D_MODEL = 1024
N_META = 16
A_HEADS = 8
A_DK = D_MODEL // A_HEADS
A_CHUNK = 64
B_HEADS = 16
B_HDIM = D_MODEL // B_HEADS
D_FF = 2752
EPS = 1e-6

LANES = 128
SUBLANES = 8
MXU_DIM = 256
D_FF_PAD = -(-D_FF // MXU_DIM) * MXU_DIM
ATT_PAD = LANES
VMEM_LIMIT = 56 * 1024 * 1024

SEQ_TILE = 512
GLA_TILE = 256
FFN_TILE = 512
Q_TILE = 256
PROJ_COLS = MXU_DIM
GATE_COLS = MXU_DIM
CONV_ROWS = 64
CONV_COLS = 256
HEADS_PER_STEP = 4
ATT_STRIP = 16
MASKED = -1e30
LOG2E = 1.4426950408889634


def _rms(x, g):
    return x * lax.rsqrt(jnp.mean(x * x, axis=-1, keepdims=True) + EPS) * g


def _sigmoid(x):
    return 1.0 / (1.0 + jnp.exp(-x))


def _split3(x):
    hi = x.astype(BF16)
    r = x - hi.astype(F32)
    mid = r.astype(BF16)
    lo = (r - mid.astype(F32)).astype(BF16)
    return hi, mid, lo


def _tril(n):
    row = lax.broadcasted_iota(jnp.int32, (n, n), 0)
    col = lax.broadcasted_iota(jnp.int32, (n, n), 1)
    return row >= col


def _cumsum_rows(x, tril_bf16):
    hi, mid, lo = _split3(x)
    dot = functools.partial(jnp.dot, preferred_element_type=F32)
    return dot(tril_bf16, hi) + dot(tril_bf16, mid) + dot(tril_bf16, lo)


def _dot_nt(a, b):
    return lax.dot_general(a, b, (((1,), (1,)), ((), ())), preferred_element_type=F32)


def _const_spec(shape):
    nd = len(shape)
    return pl.BlockSpec(shape, lambda *_: (0,) * nd, pipeline_mode=pl.Buffered(1))


def _params():
    return pltpu.CompilerParams(dimension_semantics=("arbitrary", "arbitrary"),
                                vmem_limit_bytes=VMEM_LIMIT)


def _hgrn2_kernel(x_ref, gpre_ref, gpost_ref, lbl_ref, hg_ref, win_ref, wout_ref, s0_ref,
                  out_ref, sfin_ref, proj_s, qin_s, kin_s, kout_s, v_s, dec_s, y_s, st_s, *, ts, chunk, layer):
    t = pl.program_id(1)
    n_chunks = ts // chunk

    @pl.when(t == 0)
    def _():
        st_s[...] = s0_ref[...]

    lbl = lbl_ref[...]
    e = jnp.exp(lbl - jnp.max(lbl, axis=0, keepdims=True))
    lb = jnp.sum(e[:layer + 1], axis=0, keepdims=True) / jnp.sum(e, axis=0, keepdims=True)

    xn = _rms(x_ref[0], gpre_ref[...]).astype(BF16)
    proj_s[...] = jnp.dot(xn, win_ref[...], preferred_element_type=F32)

    tril3 = jnp.concatenate([_tril(chunk).astype(BF16)] * 3, axis=1)
    for c in range(n_chunks):
        rows = slice(c * chunk, (c + 1) * chunk)
        for c0 in range(0, D_MODEL, GATE_COLS):
            cols = slice(c0, c0 + GATE_COLS)
            lbc = lb[:, cols]
            f = lbc + (1.0 - lbc) * _sigmoid(proj_s[rows, D_MODEL + c0:D_MODEL + c0 + GATE_COLS])
            k = 1.0 - f
            b = jnp.dot(tril3, jnp.concatenate(_split3(jnp.log(f)), axis=0), preferred_element_type=F32)
            dec = jnp.exp(b[chunk - 1:chunk, :])
            k_in = k * jnp.exp(-b)
            qin_s[rows, cols] = (proj_s[rows, cols] * jnp.exp(b)).astype(BF16)
            kin_s[rows, cols] = k_in.astype(BF16)
            kout_s[rows, cols] = (k_in * dec).astype(BF16)
            v_s[rows, cols] = proj_s[rows, 2 * D_MODEL + c0:2 * D_MODEL + c0 + GATE_COLS].astype(BF16)
            dec_s[c:c + 1, cols] = dec

    gt = min(ts, GLA_TILE)
    sub_chunks = gt // chunk
    row_c = lax.broadcasted_iota(jnp.int32, (gt, gt), 0)
    col_c = lax.broadcasted_iota(jnp.int32, (gt, gt), 1)
    intra = (row_c >= col_c) & (row_c // chunk == col_c // chunk)
    vt_chunk = lax.broadcasted_iota(jnp.int32, (A_DK, gt), 1) // chunk
    hgain = hg_ref[...]
    for r0 in range(0, ts, gt):
        rows = slice(r0, r0 + gt)
        for h in range(A_HEADS):
            cols = slice(h * A_DK, (h + 1) * A_DK)
            q_in = qin_s[rows, cols]
            attn = jnp.where(intra, _dot_nt(q_in, kin_s[rows, cols]), 0.0).astype(BF16)
            o_intra = jnp.dot(attn, v_s[rows, cols], preferred_element_type=F32)
            v_t = proj_s[rows, 2 * D_MODEL + h * A_DK:2 * D_MODEL + (h + 1) * A_DK].T.astype(BF16)
            v_t_blocks = jnp.concatenate([jnp.where(vt_chunk == c, v_t, jnp.zeros_like(v_t))
                                          for c in range(sub_chunks)], axis=0)
            d_st = jnp.dot(v_t_blocks, kout_s[rows, cols], preferred_element_type=F32)
            st = st_s[h]
            o_inter = []
            for c in range(sub_chunks):
                o_inter.append(_dot_nt(q_in[c * chunk:(c + 1) * chunk], st.astype(BF16)))
                dec = dec_s[r0 // chunk + c:r0 // chunk + c + 1, cols]
                st = st * dec + d_st[c * A_DK:(c + 1) * A_DK]
            st_s[h] = st
            o = o_intra + jnp.concatenate(o_inter, axis=0)
            o = o * lax.rsqrt(jnp.mean(o * o, axis=-1, keepdims=True) + EPS)
            gg = proj_s[rows, 3 * D_MODEL + h * A_DK:3 * D_MODEL + (h + 1) * A_DK]
            y_s[rows, cols] = (o * hgain[:, cols] * (gg * _sigmoid(gg))).astype(BF16)

    mix = jnp.dot(y_s[...], wout_ref[...], preferred_element_type=F32)
    out_ref[0] = x_ref[0] + _rms(mix, gpost_ref[...])

    @pl.when(t == pl.num_programs(1) - 1)
    def _():
        sfin_ref[0] = st_s[...]


def _hgrn2_layer(h, g_pre, g_post, lb_logits, head_gain, w_in, w_out, s0, *, ts, chunk, layer):
    bsz, t_len, d = h.shape
    assert t_len % ts == 0 and ts % chunk == 0
    row = lambda v: v.reshape(1, d)
    st_shape = (A_HEADS, A_DK, A_DK)
    out, s_fin = pl.pallas_call(
        functools.partial(_hgrn2_kernel, ts=ts, chunk=chunk, layer=layer),
        grid=(bsz, t_len // ts),
        in_specs=[
            pl.BlockSpec((1, ts, d), lambda b, t: (b, t, 0)),
            _const_spec((1, d)), _const_spec((1, d)), _const_spec(lb_logits.shape), _const_spec((1, d)),
            _const_spec(w_in.shape), _const_spec(w_out.shape), _const_spec(st_shape),
        ],
        out_specs=[
            pl.BlockSpec((1, ts, d), lambda b, t: (b, t, 0)),
            pl.BlockSpec((1,) + st_shape, lambda b, t: (b, 0, 0, 0)),
        ],
        out_shape=[jax.ShapeDtypeStruct(h.shape, F32),
                   jax.ShapeDtypeStruct((bsz,) + st_shape, F32)],
        scratch_shapes=[pltpu.VMEM((ts, 4 * d), F32)] + [pltpu.VMEM((ts, d), BF16)] * 4
                       + [pltpu.VMEM((max(ts // chunk, SUBLANES), d), F32), pltpu.VMEM((ts, d), BF16),
                          pltpu.VMEM(st_shape, F32)],
        compiler_params=_params(),
        name="hgrn2_mixer",
    )(h, row(g_pre), row(g_post), lb_logits, row(head_gain), w_in, w_out, s0)
    return out, s_fin


def _ffn_kernel(*refs, ts, with_attn):
    if with_attn:
        (h_ref, o_ref, wo_ref, gmix_ref, g2_ref, g3_ref, wup_ref, conv_ref, wdown_ref, tail0_ref,
         out_ref, tail_ref, u_a, u_b, tail_s, act_s) = refs
    else:
        (h_ref, g2_ref, g3_ref, wup_ref, conv_ref, wdown_ref, tail0_ref,
         out_ref, tail_ref, u_a, u_b, tail_s, act_s) = refs
    t = pl.program_id(1)
    halo = SUBLANES

    @pl.when(t == 0)
    def _():
        tail_s[...] = tail0_ref[...]

    h = h_ref[0]
    if with_attn:
        mix = jnp.dot(o_ref[0], wo_ref[...], preferred_element_type=F32)
        h = h + _rms(mix, gmix_ref[...])
    xn = _rms(h, g2_ref[...]).astype(BF16)
    rb = min(CONV_ROWS, ts)
    for j in range(D_FF_PAD // CONV_COLS):
        u_s = (u_a, u_b)[j % 2]
        for half, c0 in enumerate((j * CONV_COLS, D_FF_PAD + j * CONV_COLS)):
            here = slice(half * CONV_COLS, (half + 1) * CONV_COLS)
            u_s[0:halo, here] = tail_s[:, c0:c0 + CONV_COLS]
            u_s[halo:halo + ts, here] = jnp.dot(xn, wup_ref[:, c0:c0 + CONV_COLS], preferred_element_type=F32)
            tail_s[:, c0:c0 + CONV_COLS] = u_s[ts:ts + halo, here]
        for r0 in range(0, ts, rb):
            def conv(half, c0):
                w = conv_ref[:, c0:c0 + CONV_COLS]
                blk = u_s[r0:r0 + rb + halo, half * CONV_COLS:(half + 1) * CONV_COLS]
                tap = lambda d: (pltpu.roll(blk, d, 0) if d else blk)[halo:halo + rb]
                return w[0:1] * tap(2) + w[1:2] * tap(1) + w[2:3] * tap(0)
            cg, cv = conv(0, j * CONV_COLS), conv(1, D_FF_PAD + j * CONV_COLS)
            act_s[r0:r0 + rb, j * CONV_COLS:(j + 1) * CONV_COLS] = (cg * _sigmoid(cg) * cv).astype(BF16)

    ff = jnp.dot(act_s[...], wdown_ref[...], preferred_element_type=F32)
    out_ref[0] = h + _rms(ff, g3_ref[...])

    @pl.when(t == pl.num_programs(1) - 1)
    def _():
        tail_ref[0] = tail_s[...]


def _ffn_layer(h, g2, g3, w_up, conv_w, w_down, tail0, *, ts, attn=None):
    bsz, t_len, d = h.shape
    assert t_len % ts == 0
    row = lambda v: v.reshape(1, d)
    tile = pl.BlockSpec((1, ts, d), lambda b, t: (b, t, 0))
    args, specs = [h], [tile]
    if attn is not None:
        o, w_o, g_mix = attn
        args += [o, w_o, row(g_mix)]
        specs += [tile, _const_spec(w_o.shape), _const_spec((1, d))]
    args += [row(g2), row(g3), w_up, conv_w, w_down, tail0]
    specs += [_const_spec((1, d)), _const_spec((1, d)), _const_spec(w_up.shape), _const_spec(conv_w.shape),
              _const_spec(w_down.shape), _const_spec(tail0.shape)]
    out, tail = pl.pallas_call(
        functools.partial(_ffn_kernel, ts=ts, with_attn=attn is not None),
        grid=(bsz, t_len // ts),
        in_specs=specs,
        out_specs=[tile, pl.BlockSpec((1, SUBLANES, 2 * D_FF_PAD), lambda b, t: (b, 0, 0))],
        out_shape=[jax.ShapeDtypeStruct(h.shape, F32),
                   jax.ShapeDtypeStruct((bsz, SUBLANES, 2 * D_FF_PAD), F32)],
        scratch_shapes=[pltpu.VMEM((ts + SUBLANES, 2 * CONV_COLS), F32)] * 2
                       + [pltpu.VMEM((SUBLANES, 2 * D_FF_PAD), F32), pltpu.VMEM((ts, D_FF_PAD), BF16)],
        compiler_params=_params(),
        name="conv_ffn",
    )(*args)
    return out, tail


def _fox_proj_kernel(h_ref, gq_ref, gkv_ref, wq_ref, wk_ref, wv_ref, wf2_ref, fgb_ref, sel_ref,
                     c0_ref, q_out, k_out, v_out, cend_ref, carry_s, *, ts):
    t = pl.program_id(1)

    @pl.when(t == 0)
    def _():
        carry_s[...] = c0_ref[...]

    h = h_ref[0]
    y = h * lax.rsqrt(jnp.mean(h * h, axis=-1, keepdims=True) + EPS)
    xq = (y * gq_ref[...]).astype(BF16)
    xkv = y * gkv_ref[...]
    xkv_hi = xkv.astype(BF16)
    xkv_lo = (xkv - xkv_hi.astype(F32)).astype(BF16)
    dot = functools.partial(jnp.dot, preferred_element_type=F32)
    zf2 = dot(xkv_hi, wf2_ref[...])
    zf = (zf2[:, :LANES] + zf2[:, LANES:] + dot(xkv_lo, wf2_ref[:, :LANES])
          + fgb_ref[...])
    ls = jnp.minimum(zf, 0.0) - jnp.log(1.0 + jnp.exp(-jnp.abs(zf)))
    c = carry_s[...] + _cumsum_rows(ls, _tril(ts).astype(BF16))
    carry_s[...] = c[ts - 1:ts, :]

    c2 = c * np.float32(LOG2E)
    c_hi = c2.astype(BF16).astype(F32)
    r = c2 - c_hi
    c_mid = r.astype(BF16).astype(F32)
    c_lo = (r - c_mid).astype(BF16).astype(F32)
    lane = lax.broadcasted_iota(jnp.int32, (ts, LANES), 1)
    parts = jnp.where(lane < B_HEADS, c_hi,
                      jnp.where(lane < 2 * B_HEADS, pltpu.roll(c_mid, B_HEADS, 1),
                                jnp.where(lane < 3 * B_HEADS, pltpu.roll(c_lo, 2 * B_HEADS, 1), 0.0)))
    parts = parts.astype(BF16)
    data = lane < B_HDIM
    plus = lane < B_HDIM + 3
    minus = lane < B_HDIM + 6
    heads_per_block = PROJ_COLS // B_HDIM
    for c0 in range(0, D_MODEL, PROJ_COLS):
        cols = slice(c0, c0 + PROJ_COLS)
        qb = dot(xq, wq_ref[:, cols]) * np.float32(LOG2E / np.sqrt(B_HDIM))
        kb = dot(xkv_hi, wk_ref[:, cols])
        vb = dot(xkv_hi, wv_ref[:, cols])
        hd0 = c0 // B_HDIM
        auxb = dot(parts, sel_ref[:, hd0 * ATT_PAD:(hd0 + heads_per_block) * ATT_PAD])
        for j in range(heads_per_block):
            def head(x):
                x = x[:, (j // 2) * LANES:(j // 2 + 1) * LANES]
                return pltpu.roll(x, B_HDIM, 1) if j % 2 else x
            a = auxb[:, j * ATT_PAD:(j + 1) * ATT_PAD]
            hd = hd0 + j
            q_out[0, hd] = jnp.where(data, head(qb), jnp.where(plus, a, jnp.where(minus, 1.0, 0.0))).astype(BF16)
            k_out[0, hd] = jnp.where(data, head(kb), jnp.where(plus, 1.0, jnp.where(minus, a, 0.0))).astype(BF16)
            v_out[0, hd] = jnp.where(data, head(vb), jnp.where(lane == B_HDIM, 1.0, 0.0)).astype(BF16)

    @pl.when(t == pl.num_programs(1) - 1)
    def _():
        cend_ref[0] = carry_s[...]


def _fox_sel_matrix():
    sel = np.zeros((LANES, B_HEADS * ATT_PAD), np.float32)
    for term in range(3):
        for hd in range(B_HEADS):
            sel[term * B_HEADS + hd, hd * ATT_PAD + B_HDIM + term] = 1.0
            sel[term * B_HEADS + hd, hd * ATT_PAD + B_HDIM + 3 + term] = -1.0
    return jnp.asarray(sel, BF16)


def _fox_proj(h, g_q, g_kv, w_q, w_k, w_v, w_f2, fg_b, c0, *, ts):
    bsz, t_len, d = h.shape
    assert t_len % ts == 0
    row = lambda v: v.reshape(1, d)
    sel = _fox_sel_matrix()
    head_shape = (bsz, B_HEADS, t_len, ATT_PAD)
    head_spec = pl.BlockSpec((1, B_HEADS, ts, ATT_PAD), lambda b, t: (b, 0, t, 0))
    return pl.pallas_call(
        functools.partial(_fox_proj_kernel, ts=ts),
        grid=(bsz, t_len // ts),
        in_specs=[
            pl.BlockSpec((1, ts, d), lambda b, t: (b, t, 0)),
            _const_spec((1, d)), _const_spec((1, d)),
            _const_spec(w_q.shape), _const_spec(w_k.shape), _const_spec(w_v.shape),
            _const_spec(w_f2.shape), _const_spec(fg_b.shape),
            _const_spec(sel.shape), _const_spec(c0.shape),
        ],
        out_specs=[head_spec, head_spec, head_spec,
                   pl.BlockSpec((1, 1, LANES), lambda b, t: (b, 0, 0))],
        out_shape=[jax.ShapeDtypeStruct(head_shape, BF16)] * 3 + [jax.ShapeDtypeStruct((bsz, 1, LANES), F32)],
        scratch_shapes=[pltpu.VMEM((1, LANES), F32)],
        compiler_params=_params(),
        name="fox_proj",
    )(h, row(g_q), row(g_kv), w_q, w_k, w_v, w_f2, fg_b, sel, c0)


def _fox_attn_kernel(*refs, tq, hp, n_meta):
    if n_meta:
        q_ref, k_ref, v_ref, km_ref, vm_ref, o_ref, kcat_s, vcat_s, s_a, s_b, p_a, p_b = refs
    else:
        q_ref, k_ref, v_ref, o_ref, kcat_s, vcat_s, s_a, s_b, p_a, p_b = refs
    t_len = q_ref.shape[2]
    n_keys = n_meta + t_len
    w_max = kcat_s.shape[1]
    for hh in range(hp):
        if n_meta:
            kcat_s[hh, 0:n_meta] = km_ref[0, hh]
            vcat_s[hh, 0:n_meta] = vm_ref[0, hh]
        kcat_s[hh, n_meta:n_keys] = k_ref[0, hh]
        vcat_s[hh, n_meta:n_keys] = v_ref[0, hh]
        if w_max > n_keys:
            vcat_s[hh, n_keys:w_max] = jnp.zeros((w_max - n_keys, vcat_s.shape[2]), BF16)

    units = [(i, hh) for i in range(-(-n_keys // tq)) for hh in range(hp)]

    def geometry(i):
        k0 = i * tq
        a, e = max(n_meta, k0), min(k0 + tq, n_keys)
        return k0, a, e, e - a, -(-e // LANES) * LANES

    def logit_pieces(u):
        (i, hh), s_s = units[u], (s_a, s_b)[u % 2]
        k0, a, e, nr, w = geometry(i)
        diag_ok = (lax.broadcasted_iota(jnp.int32, (nr, e - k0), 1)
                   <= lax.broadcasted_iota(jnp.int32, (nr, e - k0), 0) + (a - k0))

        def piece(kt):
            def run():
                q = q_ref[0, hh, a - n_meta:e - n_meta, :]
                if kt < k0:
                    s_s[0:nr, kt:kt + tq] = _dot_nt(q, kcat_s[hh, kt:kt + tq])
                else:
                    s_s[0:nr, k0:e] = jnp.where(diag_ok, _dot_nt(q, kcat_s[hh, k0:e]), MASKED)
                    if w > e:
                        s_s[0:nr, e:w] = jnp.full((nr, w - e), MASKED, F32)
            return run
        return [piece(kt) for kt in range(0, e, tq)]

    def softmax_pieces(u):
        (i, hh), s_s, p_s = units[u], (s_a, s_b)[u % 2], (p_a, p_b)[u % 2]
        k0, a, e, nr, w = geometry(i)
        rs = min(nr, ATT_STRIP)

        def piece(r):
            def run():
                x = s_s[r:r + rs, 0:w]
                m = jnp.max(x, axis=-1, keepdims=True)
                p_s[r:r + rs, 0:w] = jnp.exp2(x - m).astype(BF16)
            return run
        return [piece(r) for r in range(0, nr, rs)]

    def value_pieces(u):
        (i, hh), p_s = units[u], (p_a, p_b)[u % 2]
        k0, a, e, nr, w = geometry(i)

        def run():
            acc = jnp.dot(p_s[0:nr, 0:w], vcat_s[hh, 0:w], preferred_element_type=F32)
            o = acc[:, :B_HDIM] / acc[:, B_HDIM:B_HDIM + 1]
            o_ref[0, a - n_meta:e - n_meta, hh * B_HDIM:(hh + 1) * B_HDIM] = o.astype(BF16)
        return [run]

    for step in range(len(units) + 2):
        streams = [f(u) for f, u in ((value_pieces, step - 2), (softmax_pieces, step - 1), (logit_pieces, step))
                   if 0 <= u < len(units)]
        keyed = [((j + 0.5) / len(st), n, run) for n, st in enumerate(streams) for j, run in enumerate(st)]
        for _, _, run in sorted(keyed, key=lambda t: t[:2]):
            run()


def _fox_attn(q, k, v, meta_kv, *, tq, hp=HEADS_PER_STEP):
    bsz, heads, t_len, pad = q.shape
    assert heads % hp == 0
    head_spec = pl.BlockSpec((1, hp, t_len, pad), lambda b, g: (b, g, 0, 0))
    args, specs = [q, k, v], [head_spec] * 3
    n_meta = 0
    if meta_kv is not None:
        n_meta = meta_kv[0].shape[2]
        args += list(meta_kv)
        specs += [pl.BlockSpec((1, hp, n_meta, pad), lambda b, g: (0, g, 0, 0))] * 2
    w_max = -(-(n_meta + t_len) // LANES) * LANES
    return pl.pallas_call(
        functools.partial(_fox_attn_kernel, tq=tq, hp=hp, n_meta=n_meta),
        grid=(bsz, heads // hp),
        in_specs=specs,
        out_specs=pl.BlockSpec((1, t_len, hp * B_HDIM), lambda b, g: (b, 0, g)),
        out_shape=jax.ShapeDtypeStruct((bsz, t_len, heads * B_HDIM), BF16),
        scratch_shapes=[pltpu.VMEM((hp, w_max, pad), BF16)] * 2
                       + [pltpu.VMEM((tq, w_max), F32)] * 2 + [pltpu.VMEM((tq, w_max), BF16)] * 2,
        compiler_params=_params(),
        name="fox_attention",
    )(*args)


def _pad_cols(w, n):
    return jnp.pad(w, ((0, 0), (0, n - w.shape[1])))


def _prep_ffn(w_up, conv_w, w_down):
    up = jnp.concatenate([_pad_cols(w_up[:, :D_FF], D_FF_PAD), _pad_cols(w_up[:, D_FF:], D_FF_PAD)], axis=1)
    cw = jnp.concatenate([_pad_cols(conv_w[:, :D_FF], D_FF_PAD), _pad_cols(conv_w[:, D_FF:], D_FF_PAD)], axis=1)
    down = jnp.pad(w_down, ((0, D_FF_PAD - D_FF), (0, 0)))
    return up.astype(BF16), cw.astype(F32), down.astype(BF16)


def _trunk(h, params, carry_in, *, ts, ts_ffn, chunk, tq):
    s0, tail_a, tail_b, c0, meta_kv = carry_in
    p = params
    h1, s_fin = _hgrn2_layer(h, p["g"][0, 0], p["g"][0, 1], p["lb_logits"], p["head_gain"], p["w_in"],
                             p["a_w_out"], s0, ts=ts, chunk=chunk, layer=0)
    h2, tail_a_out = _ffn_layer(h1, p["g"][0, 2], p["g"][0, 3], *p["ffn"][0], tail_a, ts=ts_ffn)
    q, k, v, c_end = _fox_proj(h2, p["g"][1, 0], p["kv_norm"], p["w_q"], p["w_k"], p["w_v"],
                               p["w_f2"], p["fg_b"], c0, ts=ts)
    o = _fox_attn(q, k, v, meta_kv, tq=tq)
    h4, tail_b_out = _ffn_layer(h2, p["g"][1, 2], p["g"][1, 3], *p["ffn"][1], tail_b, ts=ts_ffn,
                                attn=(o, p["b_w_out"], p["g"][1, 1]))
    return h4, (s_fin[0], tail_a_out[0], tail_b_out[0], c_end[0], (k[:1], v[:1]))


def kernel(x, meta_tokens, norm_gains, a_w_in, a_lb_logits, a_head_norm, a_w_out, kv_norm, kv_w, fg_b,
           b_w_q, b_w_out, ffn_w_up, ffn_conv, ffn_w_down):
    d = D_MODEL
    assert x.shape[2] == d and x.shape[1] % SEQ_TILE == 0 and meta_tokens.shape == (N_META, d)
    assert norm_gains.shape[0] == 2 and a_w_in.shape[0] == 1 and b_w_q.shape[0] == 1

    w_f = jnp.pad(kv_w[:, 2 * d:], ((0, 0), (0, LANES - B_HEADS))).astype(F32)
    w_f_hi = w_f.astype(BF16)
    params = {
        "g": norm_gains.astype(F32),
        "lb_logits": a_lb_logits.astype(F32),
        "head_gain": a_head_norm[0].astype(F32),
        "w_in": a_w_in[0].astype(BF16),
        "a_w_out": a_w_out[0].astype(BF16),
        "kv_norm": kv_norm.astype(F32),
        "w_q": b_w_q[0].astype(BF16),
        "w_k": kv_w[:, :d].astype(BF16),
        "w_v": kv_w[:, d:2 * d].astype(BF16),
        "w_f2": jnp.concatenate([w_f_hi, (w_f - w_f_hi.astype(F32)).astype(BF16)], axis=1),
        "fg_b": jnp.pad(fg_b.astype(F32), (0, LANES - B_HEADS)).reshape(1, LANES),
        "b_w_out": b_w_out[0].astype(BF16),
        "ffn": [_prep_ffn(ffn_w_up[l], ffn_conv[l], ffn_w_down[l]) for l in range(2)],
    }

    zero_carry = (jnp.zeros((A_HEADS, A_DK, A_DK), F32),
                  jnp.zeros((SUBLANES, 2 * D_FF_PAD), F32), jnp.zeros((SUBLANES, 2 * D_FF_PAD), F32),
                  jnp.zeros((1, LANES), F32), None)
    _, meta_carry = _trunk(meta_tokens[None].astype(F32), params, zero_carry,
                           ts=N_META, ts_ffn=N_META, chunk=N_META, tq=N_META)
    out, _ = _trunk(x, params, meta_carry, ts=SEQ_TILE, ts_ffn=FFN_TILE, chunk=A_CHUNK, tq=Q_TILE)
    return out
```

```python
import functools

import numpy as np
import jax
import jax.numpy as jnp
from jax import lax
from jax.experimental import pallas as pl
from jax.experimental.pallas import tpu as pltpu

F32 = jnp.float32
BF16 = jnp.bfloat16

D_MODEL = 1024
N_META = 16
A_HEADS = 8
A_DK = D_MODEL // A_HEADS
A_CHUNK = 64
B_HEADS = 16
B_HDIM = D_MODEL // B_HEADS
D_FF = 2752
EPS = 1e-6

LANES = 128
SUBLANES = 8
MXU_DIM = 256
D_FF_PAD = -(-D_FF // MXU_DIM) * MXU_DIM
ATT_PAD = LANES
AUX_LANES = LANES // B_HEADS
VMEM_LIMIT = 56 * 1024 * 1024

SEQ_TILE = 512
GLA_TILE = 256
FFN_TILE = 512
Q_TILE = 256
PROJ_COLS = MXU_DIM
GATE_COLS = MXU_DIM
CONV_ROWS = 64
CONV_COLS = 256
HEADS_PER_STEP = 4
ATT_STRIP = 16
MASKED = -1e30
LOG2E = 1.4426950408889634


def _rms(x, g):
    return x * lax.rsqrt(jnp.mean(x * x, axis=-1, keepdims=True) + EPS) * g


def _sigmoid(x):
    return 1.0 / (1.0 + jnp.exp(-x))


def _split3(x):
    hi = x.astype(BF16)
    r = x - hi.astype(F32)
    mid = r.astype(BF16)
    lo = (r - mid.astype(F32)).astype(BF16)
    return hi, mid, lo


def _tril(n):
    row = lax.broadcasted_iota(jnp.int32, (n, n), 0)
    col = lax.broadcasted_iota(jnp.int32, (n, n), 1)
    return row >= col


def _cumsum_rows(x, tril_bf16):
    hi, mid, lo = _split3(x)
    dot = functools.partial(jnp.dot, preferred_element_type=F32)
    return dot(tril_bf16, hi) + dot(tril_bf16, mid) + dot(tril_bf16, lo)


def _dot_nt(a, b):
    return lax.dot_general(a, b, (((1,), (1,)), ((), ())), preferred_element_type=F32)


def _const_spec(shape):
    nd = len(shape)
    return pl.BlockSpec(shape, lambda *_: (0,) * nd, pipeline_mode=pl.Buffered(1))


def _params():
    return pltpu.CompilerParams(dimension_semantics=("arbitrary", "arbitrary"),
                                vmem_limit_bytes=VMEM_LIMIT)


def _hgrn2_kernel(x_ref, gpre_ref, gpost_ref, lbl_ref, hg_ref, win_ref, wout_ref, s0_ref,
                  out_ref, sfin_ref, proj_s, qin_s, kin_s, kout_s, v_s, dec_s, y_s, st_s, *, ts, chunk, layer):
    t = pl.program_id(1)
    n_chunks = ts // chunk

    @pl.when(t == 0)
    def _():
        st_s[...] = s0_ref[...]

    lbl = lbl_ref[...]
    e = jnp.exp(lbl - jnp.max(lbl, axis=0, keepdims=True))
    lb = jnp.sum(e[:layer + 1], axis=0, keepdims=True) / jnp.sum(e, axis=0, keepdims=True)

    xn = _rms(x_ref[0], gpre_ref[...]).astype(BF16)
    proj_s[...] = jnp.dot(xn, win_ref[...], preferred_element_type=F32)

    tril3 = jnp.concatenate([_tril(chunk).astype(BF16)] * 3, axis=1)
    for c in range(n_chunks):
        rows = slice(c * chunk, (c + 1) * chunk)
        for c0 in range(0, D_MODEL, GATE_COLS):
            cols = slice(c0, c0 + GATE_COLS)
            lbc = lb[:, cols]
            f = lbc + (1.0 - lbc) * _sigmoid(proj_s[rows, D_MODEL + c0:D_MODEL + c0 + GATE_COLS])
            k = 1.0 - f
            b = jnp.dot(tril3, jnp.concatenate(_split3(jnp.log(f)), axis=0), preferred_element_type=F32)
            dec = jnp.exp(b[chunk - 1:chunk, :])
            k_in = k * jnp.exp(-b)
            qin_s[rows, cols] = (proj_s[rows, cols] * jnp.exp(b)).astype(BF16)
            kin_s[rows, cols] = k_in.astype(BF16)
            kout_s[rows, cols] = (k_in * dec).astype(BF16)
            v_s[rows, cols] = proj_s[rows, 2 * D_MODEL + c0:2 * D_MODEL + c0 + GATE_COLS].astype(BF16)
            dec_s[c:c + 1, cols] = dec

    gt = min(ts, GLA_TILE)
    sub_chunks = gt // chunk
    row_c = lax.broadcasted_iota(jnp.int32, (gt, gt), 0)
    col_c = lax.broadcasted_iota(jnp.int32, (gt, gt), 1)
    intra = (row_c >= col_c) & (row_c // chunk == col_c // chunk)
    vt_chunk = lax.broadcasted_iota(jnp.int32, (A_DK, gt), 1) // chunk
    hgain = hg_ref[...]
    for r0 in range(0, ts, gt):
        rows = slice(r0, r0 + gt)
        for h in range(A_HEADS):
            cols = slice(h * A_DK, (h + 1) * A_DK)
            q_in = qin_s[rows, cols]
            attn = jnp.where(intra, _dot_nt(q_in, kin_s[rows, cols]), 0.0).astype(BF16)
            o_intra = jnp.dot(attn, v_s[rows, cols], preferred_element_type=F32)
            v_t = proj_s[rows, 2 * D_MODEL + h * A_DK:2 * D_MODEL + (h + 1) * A_DK].T.astype(BF16)
            v_t_blocks = jnp.concatenate([jnp.where(vt_chunk == c, v_t, jnp.zeros_like(v_t))
                                          for c in range(sub_chunks)], axis=0)
            d_st = jnp.dot(v_t_blocks, kout_s[rows, cols], preferred_element_type=F32)
            st = st_s[h]
            o_inter = []
            for c in range(sub_chunks):
                o_inter.append(_dot_nt(q_in[c * chunk:(c + 1) * chunk], st.astype(BF16)))
                dec = dec_s[r0 // chunk + c:r0 // chunk + c + 1, cols]
                st = st * dec + d_st[c * A_DK:(c + 1) * A_DK]
            st_s[h] = st
            o = o_intra + jnp.concatenate(o_inter, axis=0)
            o = o * lax.rsqrt(jnp.mean(o * o, axis=-1, keepdims=True) + EPS)
            gg = proj_s[rows, 3 * D_MODEL + h * A_DK:3 * D_MODEL + (h + 1) * A_DK]
            y_s[rows, cols] = (o * hgain[:, cols] * (gg * _sigmoid(gg))).astype(BF16)

    mix = jnp.dot(y_s[...], wout_ref[...], preferred_element_type=F32)
    out_ref[0] = x_ref[0] + _rms(mix, gpost_ref[...])

    @pl.when(t == pl.num_programs(1) - 1)
    def _():
        sfin_ref[0] = st_s[...]


def _hgrn2_layer(h, g_pre, g_post, lb_logits, head_gain, w_in, w_out, s0, *, ts, chunk, layer):
    bsz, t_len, d = h.shape
    assert t_len % ts == 0 and ts % chunk == 0
    row = lambda v: v.reshape(1, d)
    st_shape = (A_HEADS, A_DK, A_DK)
    out, s_fin = pl.pallas_call(
        functools.partial(_hgrn2_kernel, ts=ts, chunk=chunk, layer=layer),
        grid=(bsz, t_len // ts),
        in_specs=[
            pl.BlockSpec((1, ts, d), lambda b, t: (b, t, 0)),
            _const_spec((1, d)), _const_spec((1, d)), _const_spec(lb_logits.shape), _const_spec((1, d)),
            _const_spec(w_in.shape), _const_spec(w_out.shape), _const_spec(st_shape),
        ],
        out_specs=[
            pl.BlockSpec((1, ts, d), lambda b, t: (b, t, 0)),
            pl.BlockSpec((1,) + st_shape, lambda b, t: (b, 0, 0, 0)),
        ],
        out_shape=[jax.ShapeDtypeStruct(h.shape, F32),
                   jax.ShapeDtypeStruct((bsz,) + st_shape, F32)],
        scratch_shapes=[pltpu.VMEM((ts, 4 * d), F32)] + [pltpu.VMEM((ts, d), BF16)] * 4
                       + [pltpu.VMEM((max(ts // chunk, SUBLANES), d), F32), pltpu.VMEM((ts, d), BF16),
                          pltpu.VMEM(st_shape, F32)],
        compiler_params=_params(),
        name="hgrn2_mixer",
    )(h, row(g_pre), row(g_post), lb_logits, row(head_gain), w_in, w_out, s0)
    return out, s_fin


def _ffn_kernel(*refs, ts, with_attn):
    if with_attn:
        (h_ref, o_ref, wo_ref, gmix_ref, g2_ref, g3_ref, wup_ref, conv_ref, wdown_ref, tail0_ref,
         out_ref, tail_ref, u_s, act_s) = refs
    else:
        (h_ref, g2_ref, g3_ref, wup_ref, conv_ref, wdown_ref, tail0_ref,
         out_ref, tail_ref, u_s, act_s) = refs
    t = pl.program_id(1)
    halo = SUBLANES

    @pl.when(t == 0)
    def _():
        u_s[0:halo, :] = tail0_ref[...]

    h = h_ref[0]
    if with_attn:
        mix = jnp.dot(o_ref[0], wo_ref[...], preferred_element_type=F32)
        h = h + _rms(mix, gmix_ref[...])
    xn = _rms(h, g2_ref[...]).astype(BF16)
    rb = min(CONV_ROWS, ts)
    for j in range(D_FF_PAD // CONV_COLS):
        starts = (j * CONV_COLS, D_FF_PAD + j * CONV_COLS)
        for c0 in starts:
            u_s[halo:halo + ts, c0:c0 + CONV_COLS] = jnp.dot(
                xn, wup_ref[:, c0:c0 + CONV_COLS], preferred_element_type=F32)
        for r0 in range(0, ts, rb):
            def conv(c0):
                w = conv_ref[:, c0:c0 + CONV_COLS]
                blk = u_s[r0:r0 + rb + halo, c0:c0 + CONV_COLS]
                tap = lambda d: (pltpu.roll(blk, d, 0) if d else blk)[halo:halo + rb]
                return w[0:1] * tap(2) + w[1:2] * tap(1) + w[2:3] * tap(0)
            cg, cv = conv(starts[0]), conv(starts[1])
            act_s[r0:r0 + rb, j * CONV_COLS:(j + 1) * CONV_COLS] = (cg * _sigmoid(cg) * cv).astype(BF16)
    u_s[0:halo, :] = u_s[ts:ts + halo, :]

    ff = jnp.dot(act_s[...], wdown_ref[...], preferred_element_type=F32)
    out_ref[0] = h + _rms(ff, g3_ref[...])

    @pl.when(t == pl.num_programs(1) - 1)
    def _():
        tail_ref[0] = u_s[0:halo, :]


def _ffn_layer(h, g2, g3, w_up, conv_w, w_down, tail0, *, ts, attn=None):
    bsz, t_len, d = h.shape
    assert t_len % ts == 0
    row = lambda v: v.reshape(1, d)
    tile = pl.BlockSpec((1, ts, d), lambda b, t: (b, t, 0))
    args, specs = [h], [tile]
    if attn is not None:
        o, w_o, g_mix = attn
        args += [o, w_o, row(g_mix)]
        specs += [tile, _const_spec(w_o.shape), _const_spec((1, d))]
    args += [row(g2), row(g3), w_up, conv_w, w_down, tail0]
    specs += [_const_spec((1, d)), _const_spec((1, d)), _const_spec(w_up.shape), _const_spec(conv_w.shape),
              _const_spec(w_down.shape), _const_spec(tail0.shape)]
    out, tail = pl.pallas_call(
        functools.partial(_ffn_kernel, ts=ts, with_attn=attn is not None),
        grid=(bsz, t_len // ts),
        in_specs=specs,
        out_specs=[tile, pl.BlockSpec((1, SUBLANES, 2 * D_FF_PAD), lambda b, t: (b, 0, 0))],
        out_shape=[jax.ShapeDtypeStruct(h.shape, F32),
                   jax.ShapeDtypeStruct((bsz, SUBLANES, 2 * D_FF_PAD), F32)],
        scratch_shapes=[pltpu.VMEM((ts + SUBLANES, 2 * D_FF_PAD), F32), pltpu.VMEM((ts, D_FF_PAD), BF16)],
        compiler_params=_params(),
        name="conv_ffn",
    )(*args)
    return out, tail


def _fox_proj_kernel(h_ref, gq_ref, gkv_ref, wq_ref, wk_ref, wv_ref, wf2_ref, fgb_ref, sel_ref,
                     c0_ref, q_out, k_out, v_out, cend_ref, carry_s, *, ts):
    t = pl.program_id(1)

    @pl.when(t == 0)
    def _():
        carry_s[...] = c0_ref[...]

    h = h_ref[0]
    y = h * lax.rsqrt(jnp.mean(h * h, axis=-1, keepdims=True) + EPS)
    xq = (y * gq_ref[...]).astype(BF16)
    xkv = y * gkv_ref[...]
    xkv_hi = xkv.astype(BF16)
    xkv_lo = (xkv - xkv_hi.astype(F32)).astype(BF16)
    dot = functools.partial(jnp.dot, preferred_element_type=F32)
    zf2 = dot(xkv_hi, wf2_ref[...])
    zf = (zf2[:, :LANES] + zf2[:, LANES:] + dot(xkv_lo, wf2_ref[:, :LANES])
          + fgb_ref[...])
    ls = jnp.minimum(zf, 0.0) - jnp.log(1.0 + jnp.exp(-jnp.abs(zf)))
    c = carry_s[...] + _cumsum_rows(ls, _tril(ts).astype(BF16))
    carry_s[...] = c[ts - 1:ts, :]

    c2 = c * np.float32(LOG2E)
    c_hi = c2.astype(BF16).astype(F32)
    r = c2 - c_hi
    c_mid = r.astype(BF16).astype(F32)
    c_lo = (r - c_mid).astype(BF16).astype(F32)
    lane = lax.broadcasted_iota(jnp.int32, (ts, LANES), 1)
    parts = jnp.where(lane < B_HEADS, c_hi,
                      jnp.where(lane < 2 * B_HEADS, pltpu.roll(c_mid, B_HEADS, 1),
                                jnp.where(lane < 3 * B_HEADS, pltpu.roll(c_lo, 2 * B_HEADS, 1), 0.0)))
    aux = dot(parts.astype(BF16), sel_ref[...])
    data = lane < B_HDIM
    plus = lane < B_HDIM + 3
    minus = lane < B_HDIM + 6
    heads_per_block = PROJ_COLS // B_HDIM
    for c0 in range(0, D_MODEL, PROJ_COLS):
        cols = slice(c0, c0 + PROJ_COLS)
        qb = dot(xq, wq_ref[:, cols]) * np.float32(LOG2E / np.sqrt(B_HDIM))
        kb = dot(xkv_hi, wk_ref[:, cols])
        vb = dot(xkv_hi, wv_ref[:, cols])
        hd0 = c0 // B_HDIM
        for j in range(heads_per_block):
            def head(x):
                x = x[:, (j // 2) * LANES:(j // 2 + 1) * LANES]
                return pltpu.roll(x, B_HDIM, 1) if j % 2 else x
            hd = hd0 + j
            shift = (B_HDIM - AUX_LANES * hd) % LANES
            a = pltpu.roll(aux, shift, 1) if shift else aux
            q_out[0, hd] = jnp.where(data, head(qb), jnp.where(plus, a, jnp.where(minus, 1.0, 0.0))).astype(BF16)
            k_out[0, hd] = jnp.where(data, head(kb), jnp.where(plus, 1.0, jnp.where(minus, a, 0.0))).astype(BF16)
            v_out[0, hd] = jnp.where(data, head(vb), jnp.where(lane == B_HDIM, 1.0, 0.0)).astype(BF16)

    @pl.when(t == pl.num_programs(1) - 1)
    def _():
        cend_ref[0] = carry_s[...]


def _fox_sel_matrix():
    sel = np.zeros((LANES, LANES), np.float32)
    for term in range(3):
        for hd in range(B_HEADS):
            sel[term * B_HEADS + hd, AUX_LANES * hd + term] = 1.0
            sel[term * B_HEADS + hd, AUX_LANES * hd + 3 + term] = -1.0
    return jnp.asarray(sel, BF16)


def _fox_proj(h, g_q, g_kv, w_q, w_k, w_v, w_f2, fg_b, c0, *, ts):
    bsz, t_len, d = h.shape
    assert t_len % ts == 0
    row = lambda v: v.reshape(1, d)
    sel = _fox_sel_matrix()
    head_shape = (bsz, B_HEADS, t_len, ATT_PAD)
    head_spec = pl.BlockSpec((1, B_HEADS, ts, ATT_PAD), lambda b, t: (b, 0, t, 0))
    return pl.pallas_call(
        functools.partial(_fox_proj_kernel, ts=ts),
        grid=(bsz, t_len // ts),
        in_specs=[
            pl.BlockSpec((1, ts, d), lambda b, t: (b, t, 0)),
            _const_spec((1, d)), _const_spec((1, d)),
            _const_spec(w_q.shape), _const_spec(w_k.shape), _const_spec(w_v.shape),
            _const_spec(w_f2.shape), _const_spec(fg_b.shape),
            _const_spec(sel.shape), _const_spec(c0.shape),
        ],
        out_specs=[head_spec, head_spec, head_spec,
                   pl.BlockSpec((1, 1, LANES), lambda b, t: (b, 0, 0))],
        out_shape=[jax.ShapeDtypeStruct(head_shape, BF16)] * 3 + [jax.ShapeDtypeStruct((bsz, 1, LANES), F32)],
        scratch_shapes=[pltpu.VMEM((1, LANES), F32)],
        compiler_params=_params(),
        name="fox_proj",
    )(h, row(g_q), row(g_kv), w_q, w_k, w_v, w_f2, fg_b, sel, c0)


def _fox_attn_kernel(*refs, tq, hp, n_meta):
    if n_meta:
        q_ref, k_ref, v_ref, km_ref, vm_ref, o_ref, kcat_s, vcat_s, s_a, s_b, p_a, p_b = refs
    else:
        q_ref, k_ref, v_ref, o_ref, kcat_s, vcat_s, s_a, s_b, p_a, p_b = refs
    t_len = q_ref.shape[2]
    n_keys = n_meta + t_len
    w_max = kcat_s.shape[1]
    for hh in range(hp):
        if n_meta:
            kcat_s[hh, 0:n_meta] = km_ref[0, hh]
            vcat_s[hh, 0:n_meta] = vm_ref[0, hh]
        kcat_s[hh, n_meta:n_keys] = k_ref[0, hh]
        vcat_s[hh, n_meta:n_keys] = v_ref[0, hh]
        if w_max > n_keys:
            vcat_s[hh, n_keys:w_max] = jnp.zeros((w_max - n_keys, vcat_s.shape[2]), BF16)

    units = [(i, hh) for i in range(-(-n_keys // tq)) for hh in range(hp)]

    def geometry(i):
        k0 = i * tq
        a, e = max(n_meta, k0), min(k0 + tq, n_keys)
        return k0, a, e, e - a, -(-e // LANES) * LANES

    def logit_pieces(u):
        (i, hh), s_s = units[u], (s_a, s_b)[u % 2]
        k0, a, e, nr, w = geometry(i)
        diag_ok = (lax.broadcasted_iota(jnp.int32, (nr, e - k0), 1)
                   <= lax.broadcasted_iota(jnp.int32, (nr, e - k0), 0) + (a - k0))

        def piece(kt):
            def run():
                q = q_ref[0, hh, a - n_meta:e - n_meta, :]
                if kt < k0:
                    s_s[0:nr, kt:kt + tq] = _dot_nt(q, kcat_s[hh, kt:kt + tq])
                else:
                    s_s[0:nr, k0:e] = jnp.where(diag_ok, _dot_nt(q, kcat_s[hh, k0:e]), MASKED)
                    if w > e:
                        s_s[0:nr, e:w] = jnp.full((nr, w - e), MASKED, F32)
            return run
        return [piece(kt) for kt in range(0, e, tq)]

    def softmax_pieces(u):
        (i, hh), s_s, p_s = units[u], (s_a, s_b)[u % 2], (p_a, p_b)[u % 2]
        k0, a, e, nr, w = geometry(i)
        rs = min(nr, ATT_STRIP)

        def piece(r):
            def run():
                x = s_s[r:r + rs, 0:w]
                m = jnp.max(x, axis=-1, keepdims=True)
                p_s[r:r + rs, 0:w] = jnp.exp2(x - m).astype(BF16)
            return run
        return [piece(r) for r in range(0, nr, rs)]

    def value_pieces(u):
        (i, hh), p_s = units[u], (p_a, p_b)[u % 2]
        k0, a, e, nr, w = geometry(i)

        def run():
            acc = jnp.dot(p_s[0:nr, 0:w], vcat_s[hh, 0:w], preferred_element_type=F32)
            o = acc[:, :B_HDIM] / acc[:, B_HDIM:B_HDIM + 1]
            o_ref[0, a - n_meta:e - n_meta, hh * B_HDIM:(hh + 1) * B_HDIM] = o.astype(BF16)
        return [run]

    for step in range(len(units) + 2):
        streams = [f(u) for f, u in ((value_pieces, step - 2), (softmax_pieces, step - 1), (logit_pieces, step))
                   if 0 <= u < len(units)]
        keyed = [((j + 0.5) / len(st), n, run) for n, st in enumerate(streams) for j, run in enumerate(st)]
        for _, _, run in sorted(keyed, key=lambda t: t[:2]):
            run()


def _fox_attn(q, k, v, meta_kv, *, tq, hp=HEADS_PER_STEP):
    bsz, heads, t_len, pad = q.shape
    assert heads % hp == 0
    head_spec = pl.BlockSpec((1, hp, t_len, pad), lambda b, g: (b, g, 0, 0))
    args, specs = [q, k, v], [head_spec] * 3
    n_meta = 0
    if meta_kv is not None:
        n_meta = meta_kv[0].shape[2]
        args += list(meta_kv)
        specs += [pl.BlockSpec((1, hp, n_meta, pad), lambda b, g: (0, g, 0, 0))] * 2
    w_max = -(-(n_meta + t_len) // LANES) * LANES
    return pl.pallas_call(
        functools.partial(_fox_attn_kernel, tq=tq, hp=hp, n_meta=n_meta),
        grid=(bsz, heads // hp),
        in_specs=specs,
        out_specs=pl.BlockSpec((1, t_len, hp * B_HDIM), lambda b, g: (b, 0, g)),
        out_shape=jax.ShapeDtypeStruct((bsz, t_len, heads * B_HDIM), BF16),
        scratch_shapes=[pltpu.VMEM((hp, w_max, pad), BF16)] * 2
                       + [pltpu.VMEM((tq, w_max), F32)] * 2 + [pltpu.VMEM((tq, w_max), BF16)] * 2,
        compiler_params=_params(),
        name="fox_attention",
    )(*args)


def _pad_cols(w, n):
    return jnp.pad(w, ((0, 0), (0, n - w.shape[1])))


def _prep_ffn(w_up, conv_w, w_down):
    up = jnp.concatenate([_pad_cols(w_up[:, :D_FF], D_FF_PAD), _pad_cols(w_up[:, D_FF:], D_FF_PAD)], axis=1)
    cw = jnp.concatenate([_pad_cols(conv_w[:, :D_FF], D_FF_PAD), _pad_cols(conv_w[:, D_FF:], D_FF_PAD)], axis=1)
    down = jnp.pad(w_down, ((0, D_FF_PAD - D_FF), (0, 0)))
    return up.astype(BF16), cw.astype(F32), down.astype(BF16)


def _trunk(h, params, carry_in, *, ts, ts_ffn, chunk, tq):
    s0, tail_a, tail_b, c0, meta_kv = carry_in
    p = params
    h1, s_fin = _hgrn2_layer(h, p["g"][0, 0], p["g"][0, 1], p["lb_logits"], p["head_gain"], p["w_in"],
                             p["a_w_out"], s0, ts=ts, chunk=chunk, layer=0)
    h2, tail_a_out = _ffn_layer(h1, p["g"][0, 2], p["g"][0, 3], *p["ffn"][0], tail_a, ts=ts_ffn)
    q, k, v, c_end = _fox_proj(h2, p["g"][1, 0], p["kv_norm"], p["w_q"], p["w_k"], p["w_v"],
                               p["w_f2"], p["fg_b"], c0, ts=ts)
    o = _fox_attn(q, k, v, meta_kv, tq=tq)
    h4, tail_b_out = _ffn_layer(h2, p["g"][1, 2], p["g"][1, 3], *p["ffn"][1], tail_b, ts=ts_ffn,
                                attn=(o, p["b_w_out"], p["g"][1, 1]))
    return h4, (s_fin[0], tail_a_out[0], tail_b_out[0], c_end[0], (k[:1], v[:1]))


def kernel(x, meta_tokens, norm_gains, a_w_in, a_lb_logits, a_head_norm, a_w_out, kv_norm, kv_w, fg_b,
           b_w_q, b_w_out, ffn_w_up, ffn_conv, ffn_w_down):
    d = D_MODEL
    assert x.shape[2] == d and x.shape[1] % SEQ_TILE == 0 and meta_tokens.shape == (N_META, d)
    assert norm_gains.shape[0] == 2 and a_w_in.shape[0] == 1 and b_w_q.shape[0] == 1

    w_f = jnp.pad(kv_w[:, 2 * d:], ((0, 0), (0, LANES - B_HEADS))).astype(F32)
    w_f_hi = w_f.astype(BF16)
    params = {
        "g": norm_gains.astype(F32),
        "lb_logits": a_lb_logits.astype(F32),
        "head_gain": a_head_norm[0].astype(F32),
        "w_in": a_w_in[0].astype(BF16),
        "a_w_out": a_w_out[0].astype(BF16),
        "kv_norm": kv_norm.astype(F32),
        "w_q": b_w_q[0].astype(BF16),
        "w_k": kv_w[:, :d].astype(BF16),
        "w_v": kv_w[:, d:2 * d].astype(BF16),
        "w_f2": jnp.concatenate([w_f_hi, (w_f - w_f_hi.astype(F32)).astype(BF16)], axis=1),
        "fg_b": jnp.pad(fg_b.astype(F32), (0, LANES - B_HEADS)).reshape(1, LANES),
        "b_w_out": b_w_out[0].astype(BF16),
        "ffn": [_prep_ffn(ffn_w_up[l], ffn_conv[l], ffn_w_down[l]) for l in range(2)],
    }

    zero_carry = (jnp.zeros((A_HEADS, A_DK, A_DK), F32),
                  jnp.zeros((SUBLANES, 2 * D_FF_PAD), F32), jnp.zeros((SUBLANES, 2 * D_FF_PAD), F32),
                  jnp.zeros((1, LANES), F32), None)
    _, meta_carry = _trunk(meta_tokens[None].astype(F32), params, zero_carry,
                           ts=N_META, ts_ffn=N_META, chunk=N_META, tq=N_META)
    out, _ = _trunk(x, params, meta_carry, ts=SEQ_TILE, ts_ffn=FFN_TILE, chunk=A_CHUNK, tq=Q_TILE)
    return out
```

```python
import functools

import numpy as np
import jax
import jax.numpy as jnp
from jax import lax
from jax.experimental import pallas as pl
from jax.experimental.pallas import tpu as pltpu

F32 = jnp.float32
BF16 = jnp.bfloat16

D_MODEL = 1024
N_META = 16
A_HEADS = 8
A_DK = D_MODEL // A_HEADS
A_CHUNK = 64
B_HEADS = 16
B_HDIM = D_MODEL // B_HEADS
D_FF = 2752
EPS = 1e-6

LANES = 128
SUBLANES = 8
MXU_DIM = 256
D_FF_PAD = -(-D_FF // MXU_DIM) * MXU_DIM
ATT_PAD = LANES
AUX_LANES = LANES // B_HEADS
VMEM_LIMIT = 56 * 1024 * 1024

SEQ_TILE = 512
GLA_TILE = 256
FFN_TILE = 512
Q_TILE = 256
PROJ_COLS = MXU_DIM
GATE_COLS = MXU_DIM
CONV_ROWS = 64
CONV_COLS = 256
HEADS_PER_STEP = 4
ATT_STRIP = 16
MASKED = -1e30
LOG2E = 1.4426950408889634


def _rms(x, g):
    return x * lax.rsqrt(jnp.mean(x * x, axis=-1, keepdims=True) + EPS) * g


def _sigmoid(x):
    return 1.0 / (1.0 + jnp.exp(-x))


def _split3(x):
    hi = x.astype(BF16)
    r = x - hi.astype(F32)
    mid = r.astype(BF16)
    lo = (r - mid.astype(F32)).astype(BF16)
    return hi, mid, lo


def _tril(n):
    row = lax.broadcasted_iota(jnp.int32, (n, n), 0)
    col = lax.broadcasted_iota(jnp.int32, (n, n), 1)
    return row >= col


def _cumsum_rows(x, tril_bf16):
    hi, mid, lo = _split3(x)
    dot = functools.partial(jnp.dot, preferred_element_type=F32)
    return dot(tril_bf16, hi) + dot(tril_bf16, mid) + dot(tril_bf16, lo)


def _dot_nt(a, b):
    return lax.dot_general(a, b, (((1,), (1,)), ((), ())), preferred_element_type=F32)


def _const_spec(shape):
    nd = len(shape)
    return pl.BlockSpec(shape, lambda *_: (0,) * nd, pipeline_mode=pl.Buffered(1))


def _params():
    return pltpu.CompilerParams(dimension_semantics=("arbitrary", "arbitrary"),
                                vmem_limit_bytes=VMEM_LIMIT)


def _hgrn2_kernel(x_ref, gpre_ref, gpost_ref, lbl_ref, hg_ref, win_ref, wout_ref, s0_ref,
                  out_ref, sfin_ref, proj_s, qin_s, kin_s, kout_s, v_s, dec_s, y_s, st_s, *, ts, chunk, layer):
    t = pl.program_id(1)
    n_chunks = ts // chunk

    @pl.when(t == 0)
    def _():
        st_s[...] = s0_ref[...]

    lbl = lbl_ref[...]
    e = jnp.exp(lbl - jnp.max(lbl, axis=0, keepdims=True))
    lb = jnp.sum(e[:layer + 1], axis=0, keepdims=True) / jnp.sum(e, axis=0, keepdims=True)

    xn = _rms(x_ref[0], gpre_ref[...]).astype(BF16)
    proj_s[...] = jnp.dot(xn, win_ref[...], preferred_element_type=F32)

    tril3 = jnp.concatenate([_tril(chunk).astype(BF16)] * 3, axis=1)
    if n_chunks < dec_s.shape[0]:
        dec_s[n_chunks:, :] = jnp.zeros((dec_s.shape[0] - n_chunks, D_MODEL), F32)
    for c in range(n_chunks):
        rows = slice(c * chunk, (c + 1) * chunk)
        for c0 in range(0, D_MODEL, GATE_COLS):
            cols = slice(c0, c0 + GATE_COLS)
            lbc = lb[:, cols]
            f = lbc + (1.0 - lbc) * _sigmoid(proj_s[rows, D_MODEL + c0:D_MODEL + c0 + GATE_COLS])
            k = 1.0 - f
            b = jnp.dot(tril3, jnp.concatenate(_split3(jnp.log(f)), axis=0), preferred_element_type=F32)
            dec = jnp.exp(b[chunk - 1:chunk, :])
            k_in = k * jnp.exp(-b)
            qin_s[rows, cols] = (proj_s[rows, cols] * jnp.exp(b)).astype(BF16)
            kin_s[rows, cols] = k_in.astype(BF16)
            kout_s[rows, cols] = k_in * dec
            v_s[rows, cols] = proj_s[rows, 2 * D_MODEL + c0:2 * D_MODEL + c0 + GATE_COLS].astype(BF16)
            dec_s[c:c + 1, cols] = dec

    gt = min(ts, GLA_TILE)
    sub_chunks = gt // chunk
    row_c = lax.broadcasted_iota(jnp.int32, (gt, gt), 0)
    col_c = lax.broadcasted_iota(jnp.int32, (gt, gt), 1)
    intra = (row_c >= col_c) & (row_c // chunk == col_c // chunk)
    v_chunk = lax.broadcasted_iota(jnp.int32, (gt, A_DK), 0) // chunk
    hgain = hg_ref[...]
    for r0 in range(0, ts, gt):
        rows = slice(r0, r0 + gt)
        for h in range(A_HEADS):
            cols = slice(h * A_DK, (h + 1) * A_DK)
            q_in, v = qin_s[rows, cols], v_s[rows, cols]
            attn = jnp.where(intra, _dot_nt(q_in, kin_s[rows, cols]), 0.0).astype(BF16)
            o_intra = jnp.dot(attn, v, preferred_element_type=F32)
            v_blocks = jnp.concatenate([jnp.where(v_chunk == c, v, jnp.zeros_like(v))
                                        for c in range(sub_chunks)], axis=1)
            d_st = jnp.dot(kout_s[rows, cols].T.astype(BF16), v_blocks,
                           preferred_element_type=F32)
            dec_t = dec_s[0:SUBLANES, cols].T
            st = st_s[h]
            o_inter = []
            for c in range(sub_chunks):
                o_inter.append(jnp.dot(q_in[c * chunk:(c + 1) * chunk], st.astype(BF16),
                                       preferred_element_type=F32))
                ci = r0 // chunk + c
                st = st * dec_t[:, ci:ci + 1] + d_st[:, c * A_DK:(c + 1) * A_DK]
            st_s[h] = st
            o = o_intra + jnp.concatenate(o_inter, axis=0)
            o = o * lax.rsqrt(jnp.mean(o * o, axis=-1, keepdims=True) + EPS)
            gg = proj_s[rows, 3 * D_MODEL + h * A_DK:3 * D_MODEL + (h + 1) * A_DK]
            y_s[rows, cols] = (o * hgain[:, cols] * (gg * _sigmoid(gg))).astype(BF16)

    mix = jnp.dot(y_s[...], wout_ref[...], preferred_element_type=F32)
    out_ref[0] = x_ref[0] + _rms(mix, gpost_ref[...])

    @pl.when(t == pl.num_programs(1) - 1)
    def _():
        sfin_ref[0] = st_s[...]


def _hgrn2_layer(h, g_pre, g_post, lb_logits, head_gain, w_in, w_out, s0, *, ts, chunk, layer):
    bsz, t_len, d = h.shape
    assert t_len % ts == 0 and ts % chunk == 0
    row = lambda v: v.reshape(1, d)
    st_shape = (A_HEADS, A_DK, A_DK)
    out, s_fin = pl.pallas_call(
        functools.partial(_hgrn2_kernel, ts=ts, chunk=chunk, layer=layer),
        grid=(bsz, t_len // ts),
        in_specs=[
            pl.BlockSpec((1, ts, d), lambda b, t: (b, t, 0)),
            _const_spec((1, d)), _const_spec((1, d)), _const_spec(lb_logits.shape), _const_spec((1, d)),
            _const_spec(w_in.shape), _const_spec(w_out.shape), _const_spec(st_shape),
        ],
        out_specs=[
            pl.BlockSpec((1, ts, d), lambda b, t: (b, t, 0)),
            pl.BlockSpec((1,) + st_shape, lambda b, t: (b, 0, 0, 0)),
        ],
        out_shape=[jax.ShapeDtypeStruct(h.shape, F32),
                   jax.ShapeDtypeStruct((bsz,) + st_shape, F32)],
        scratch_shapes=[pltpu.VMEM((ts, 4 * d), F32), pltpu.VMEM((ts, d), BF16), pltpu.VMEM((ts, d), BF16),
                        pltpu.VMEM((ts, d), F32), pltpu.VMEM((ts, d), BF16),
                        pltpu.VMEM((max(ts // chunk, SUBLANES), d), F32), pltpu.VMEM((ts, d), BF16),
                        pltpu.VMEM(st_shape, F32)],
        compiler_params=_params(),
        name="hgrn2_mixer",
    )(h, row(g_pre), row(g_post), lb_logits, row(head_gain), w_in, w_out, s0)
    return out, s_fin


def _ffn_kernel(*refs, ts, with_attn):
    if with_attn:
        (h_ref, o_ref, wo_ref, gmix_ref, g2_ref, g3_ref, wup_ref, conv_ref, wdown_ref, tail0_ref,
         out_ref, tail_ref, u_s, act_s) = refs
    else:
        (h_ref, g2_ref, g3_ref, wup_ref, conv_ref, wdown_ref, tail0_ref,
         out_ref, tail_ref, u_s, act_s) = refs
    t = pl.program_id(1)
    halo = SUBLANES

    @pl.when(t == 0)
    def _():
        u_s[0:halo, :] = tail0_ref[...]

    h = h_ref[0]
    if with_attn:
        mix = jnp.dot(o_ref[0], wo_ref[...], preferred_element_type=F32)
        h = h + _rms(mix, gmix_ref[...])
    xn = _rms(h, g2_ref[...]).astype(BF16)
    rb = min(CONV_ROWS, ts)
    for j in range(D_FF_PAD // CONV_COLS):
        starts = (j * CONV_COLS, D_FF_PAD + j * CONV_COLS)
        for c0 in starts:
            u_s[halo:halo + ts, c0:c0 + CONV_COLS] = jnp.dot(
                xn, wup_ref[:, c0:c0 + CONV_COLS], preferred_element_type=F32)
        for r0 in range(0, ts, rb):
            def conv(c0):
                w = conv_ref[:, c0:c0 + CONV_COLS]
                blk = u_s[r0:r0 + rb + halo, c0:c0 + CONV_COLS]
                tap = lambda d: (pltpu.roll(blk, d, 0) if d else blk)[halo:halo + rb]
                return w[0:1] * tap(2) + w[1:2] * tap(1) + w[2:3] * tap(0)
            cg, cv = conv(starts[0]), conv(starts[1])
            act_s[r0:r0 + rb, j * CONV_COLS:(j + 1) * CONV_COLS] = (cg * _sigmoid(cg) * cv).astype(BF16)
    u_s[0:halo, :] = u_s[ts:ts + halo, :]

    ff = jnp.dot(act_s[...], wdown_ref[...], preferred_element_type=F32)
    out_ref[0] = h + _rms(ff, g3_ref[...])

    @pl.when(t == pl.num_programs(1) - 1)
    def _():
        tail_ref[0] = u_s[0:halo, :]


def _ffn_layer(h, g2, g3, w_up, conv_w, w_down, tail0, *, ts, attn=None):
    bsz, t_len, d = h.shape
    assert t_len % ts == 0
    row = lambda v: v.reshape(1, d)
    tile = pl.BlockSpec((1, ts, d), lambda b, t: (b, t, 0))
    args, specs = [h], [tile]
    if attn is not None:
        o, w_o, g_mix = attn
        args += [o, w_o, row(g_mix)]
        specs += [tile, _const_spec(w_o.shape), _const_spec((1, d))]
    args += [row(g2), row(g3), w_up, conv_w, w_down, tail0]
    specs += [_const_spec((1, d)), _const_spec((1, d)), _const_spec(w_up.shape), _const_spec(conv_w.shape),
              _const_spec(w_down.shape), _const_spec(tail0.shape)]
    out, tail = pl.pallas_call(
        functools.partial(_ffn_kernel, ts=ts, with_attn=attn is not None),
        grid=(bsz, t_len // ts),
        in_specs=specs,
        out_specs=[tile, pl.BlockSpec((1, SUBLANES, 2 * D_FF_PAD), lambda b, t: (b, 0, 0))],
        out_shape=[jax.ShapeDtypeStruct(h.shape, F32),
                   jax.ShapeDtypeStruct((bsz, SUBLANES, 2 * D_FF_PAD), F32)],
        scratch_shapes=[pltpu.VMEM((ts + SUBLANES, 2 * D_FF_PAD), F32), pltpu.VMEM((ts, D_FF_PAD), BF16)],
        compiler_params=_params(),
        name="conv_ffn",
    )(*args)
    return out, tail


def _fox_proj_kernel(h_ref, gq_ref, gkv_ref, wq_ref, wk_ref, wv_ref, wf2_ref, fgb_ref, sel_ref,
                     c0_ref, q_out, k_out, v_out, cend_ref, carry_s, *, ts):
    t = pl.program_id(1)

    @pl.when(t == 0)
    def _():
        carry_s[...] = c0_ref[...]

    h = h_ref[0]
    y = h * lax.rsqrt(jnp.mean(h * h, axis=-1, keepdims=True) + EPS)
    xq = (y * gq_ref[...]).astype(BF16)
    xkv = y * gkv_ref[...]
    xkv_hi = xkv.astype(BF16)
    xkv_lo = (xkv - xkv_hi.astype(F32)).astype(BF16)
    dot = functools.partial(jnp.dot, preferred_element_type=F32)
    zf2 = dot(xkv_hi, wf2_ref[...])
    zf = (zf2[:, :LANES] + zf2[:, LANES:] + dot(xkv_lo, wf2_ref[:, :LANES])
          + fgb_ref[...])
    ls = jnp.minimum(zf, 0.0) - jnp.log(1.0 + jnp.exp(-jnp.abs(zf)))
    c = carry_s[...] + _cumsum_rows(ls, _tril(ts).astype(BF16))
    carry_s[...] = c[ts - 1:ts, :]

    c2 = c * np.float32(LOG2E)
    c_hi = c2.astype(BF16).astype(F32)
    r = c2 - c_hi
    c_mid = r.astype(BF16).astype(F32)
    c_lo = (r - c_mid).astype(BF16).astype(F32)
    lane = lax.broadcasted_iota(jnp.int32, (ts, LANES), 1)
    parts = jnp.where(lane < B_HEADS, c_hi,
                      jnp.where(lane < 2 * B_HEADS, pltpu.roll(c_mid, B_HEADS, 1),
                                jnp.where(lane < 3 * B_HEADS, pltpu.roll(c_lo, 2 * B_HEADS, 1), 0.0)))
    aux = dot(parts.astype(BF16), sel_ref[...])
    data = lane < B_HDIM
    plus = lane < B_HDIM + 3
    minus = lane < B_HDIM + 6
    heads_per_block = PROJ_COLS // B_HDIM
    for c0 in range(0, D_MODEL, PROJ_COLS):
        cols = slice(c0, c0 + PROJ_COLS)
        qb = dot(xq, wq_ref[:, cols]) * np.float32(LOG2E / np.sqrt(B_HDIM))
        kb = dot(xkv_hi, wk_ref[:, cols])
        vb = dot(xkv_hi, wv_ref[:, cols])
        hd0 = c0 // B_HDIM
        for j in range(heads_per_block):
            def head(x):
                x = x[:, (j // 2) * LANES:(j // 2 + 1) * LANES]
                return pltpu.roll(x, B_HDIM, 1) if j % 2 else x
            hd = hd0 + j
            shift = (B_HDIM - AUX_LANES * hd) % LANES
            a = pltpu.roll(aux, shift, 1) if shift else aux
            q_out[0, hd] = jnp.where(data, head(qb), jnp.where(plus, a, jnp.where(minus, 1.0, 0.0))).astype(BF16)
            k_out[0, hd] = jnp.where(data, head(kb), jnp.where(plus, 1.0, jnp.where(minus, a, 0.0))).astype(BF16)
            v_out[0, hd] = jnp.where(data, head(vb), jnp.where(lane == B_HDIM, 1.0, 0.0)).astype(BF16)

    @pl.when(t == pl.num_programs(1) - 1)
    def _():
        cend_ref[0] = carry_s[...]


def _fox_sel_matrix():
    sel = np.zeros((LANES, LANES), np.float32)
    for term in range(3):
        for hd in range(B_HEADS):
            sel[term * B_HEADS + hd, AUX_LANES * hd + term] = 1.0
            sel[term * B_HEADS + hd, AUX_LANES * hd + 3 + term] = -1.0
    return jnp.asarray(sel, BF16)


def _fox_proj(h, g_q, g_kv, w_q, w_k, w_v, w_f2, fg_b, c0, *, ts):
    bsz, t_len, d = h.shape
    assert t_len % ts == 0
    row = lambda v: v.reshape(1, d)
    sel = _fox_sel_matrix()
    head_shape = (bsz, B_HEADS, t_len, ATT_PAD)
    head_spec = pl.BlockSpec((1, B_HEADS, ts, ATT_PAD), lambda b, t: (b, 0, t, 0))
    return pl.pallas_call(
        functools.partial(_fox_proj_kernel, ts=ts),
        grid=(bsz, t_len // ts),
        in_specs=[
            pl.BlockSpec((1, ts, d), lambda b, t: (b, t, 0)),
            _const_spec((1, d)), _const_spec((1, d)),
            _const_spec(w_q.shape), _const_spec(w_k.shape), _const_spec(w_v.shape),
            _const_spec(w_f2.shape), _const_spec(fg_b.shape),
            _const_spec(sel.shape), _const_spec(c0.shape),
        ],
        out_specs=[head_spec, head_spec, head_spec,
                   pl.BlockSpec((1, 1, LANES), lambda b, t: (b, 0, 0))],
        out_shape=[jax.ShapeDtypeStruct(head_shape, BF16)] * 3 + [jax.ShapeDtypeStruct((bsz, 1, LANES), F32)],
        scratch_shapes=[pltpu.VMEM((1, LANES), F32)],
        compiler_params=_params(),
        name="fox_proj",
    )(h, row(g_q), row(g_kv), w_q, w_k, w_v, w_f2, fg_b, sel, c0)


def _fox_attn_kernel(*refs, tq, hp, n_meta):
    if n_meta:
        q_ref, k_ref, v_ref, km_ref, vm_ref, o_ref, kcat_s, vcat_s, s_a, s_b, p_a, p_b = refs
    else:
        q_ref, k_ref, v_ref, o_ref, kcat_s, vcat_s, s_a, s_b, p_a, p_b = refs
    t_len = q_ref.shape[2]
    n_keys = n_meta + t_len
    w_max = kcat_s.shape[1]
    for hh in range(hp):
        if n_meta:
            kcat_s[hh, 0:n_meta] = km_ref[0, hh]
            vcat_s[hh, 0:n_meta] = vm_ref[0, hh]
        kcat_s[hh, n_meta:n_keys] = k_ref[0, hh]
        vcat_s[hh, n_meta:n_keys] = v_ref[0, hh]
        if w_max > n_keys:
            vcat_s[hh, n_keys:w_max] = jnp.zeros((w_max - n_keys, vcat_s.shape[2]), BF16)

    units = [(i, hh) for i in range(-(-n_keys // tq)) for hh in range(hp)]

    def geometry(i):
        k0 = i * tq
        a, e = max(n_meta, k0), min(k0 + tq, n_keys)
        return k0, a, e, e - a, -(-e // LANES) * LANES

    def logit_pieces(u):
        (i, hh), s_s = units[u], (s_a, s_b)[u % 2]
        k0, a, e, nr, w = geometry(i)
        diag_ok = (lax.broadcasted_iota(jnp.int32, (nr, e - k0), 1)
                   <= lax.broadcasted_iota(jnp.int32, (nr, e - k0), 0) + (a - k0))

        def piece(kt):
            def run():
                q = q_ref[0, hh, a - n_meta:e - n_meta, :]
                if kt < k0:
                    s_s[0:nr, kt:kt + tq] = _dot_nt(q, kcat_s[hh, kt:kt + tq])
                else:
                    s_s[0:nr, k0:e] = jnp.where(diag_ok, _dot_nt(q, kcat_s[hh, k0:e]), MASKED)
                    if w > e:
                        s_s[0:nr, e:w] = jnp.full((nr, w - e), MASKED, F32)
            return run
        return [piece(kt) for kt in range(0, e, tq)]

    def softmax_pieces(u):
        (i, hh), s_s, p_s = units[u], (s_a, s_b)[u % 2], (p_a, p_b)[u % 2]
        k0, a, e, nr, w = geometry(i)
        rs = min(nr, ATT_STRIP)

        def piece(r):
            def run():
                x = s_s[r:r + rs, 0:w]
                m = jnp.max(x, axis=-1, keepdims=True)
                p_s[r:r + rs, 0:w] = jnp.exp2(x - m).astype(BF16)
            return run
        return [piece(r) for r in range(0, nr, rs)]

    def value_pieces(u):
        (i, hh), p_s = units[u], (p_a, p_b)[u % 2]
        k0, a, e, nr, w = geometry(i)

        def run():
            acc = jnp.dot(p_s[0:nr, 0:w], vcat_s[hh, 0:w], preferred_element_type=F32)
            o = acc[:, :B_HDIM] / acc[:, B_HDIM:B_HDIM + 1]
            o_ref[0, a - n_meta:e - n_meta, hh * B_HDIM:(hh + 1) * B_HDIM] = o.astype(BF16)
        return [run]

    for step in range(len(units) + 2):
        streams = [f(u) for f, u in ((value_pieces, step - 2), (softmax_pieces, step - 1), (logit_pieces, step))
                   if 0 <= u < len(units)]
        keyed = [((j + 0.5) / len(st), n, run) for n, st in enumerate(streams) for j, run in enumerate(st)]
        for _, _, run in sorted(keyed, key=lambda t: t[:2]):
            run()


def _fox_attn(q, k, v, meta_kv, *, tq, hp=HEADS_PER_STEP):
    bsz, heads, t_len, pad = q.shape
    assert heads % hp == 0
    head_spec = pl.BlockSpec((1, hp, t_len, pad), lambda b, g: (b, g, 0, 0))
    args, specs = [q, k, v], [head_spec] * 3
    n_meta = 0
    if meta_kv is not None:
        n_meta = meta_kv[0].shape[2]
        args += list(meta_kv)
        specs += [pl.BlockSpec((1, hp, n_meta, pad), lambda b, g: (0, g, 0, 0))] * 2
    w_max = -(-(n_meta + t_len) // LANES) * LANES
    return pl.pallas_call(
        functools.partial(_fox_attn_kernel, tq=tq, hp=hp, n_meta=n_meta),
        grid=(bsz, heads // hp),
        in_specs=specs,
        out_specs=pl.BlockSpec((1, t_len, hp * B_HDIM), lambda b, g: (b, 0, g)),
        out_shape=jax.ShapeDtypeStruct((bsz, t_len, heads * B_HDIM), BF16),
        scratch_shapes=[pltpu.VMEM((hp, w_max, pad), BF16)] * 2
                       + [pltpu.VMEM((tq, w_max), F32)] * 2 + [pltpu.VMEM((tq, w_max), BF16)] * 2,
        compiler_params=_params(),
        name="fox_attention",
    )(*args)


def _pad_cols(w, n):
    return jnp.pad(w, ((0, 0), (0, n - w.shape[1])))


def _prep_ffn(w_up, conv_w, w_down):
    up = jnp.concatenate([_pad_cols(w_up[:, :D_FF], D_FF_PAD), _pad_cols(w_up[:, D_FF:], D_FF_PAD)], axis=1)
    cw = jnp.concatenate([_pad_cols(conv_w[:, :D_FF], D_FF_PAD), _pad_cols(conv_w[:, D_FF:], D_FF_PAD)], axis=1)
    down = jnp.pad(w_down, ((0, D_FF_PAD - D_FF), (0, 0)))
    return up.astype(BF16), cw.astype(F32), down.astype(BF16)


def _trunk(h, params, carry_in, *, ts, ts_ffn, chunk, tq):
    s0, tail_a, tail_b, c0, meta_kv = carry_in
    p = params
    h1, s_fin = _hgrn2_layer(h, p["g"][0, 0], p["g"][0, 1], p["lb_logits"], p["head_gain"], p["w_in"],
                             p["a_w_out"], s0, ts=ts, chunk=chunk, layer=0)
    h2, tail_a_out = _ffn_layer(h1, p["g"][0, 2], p["g"][0, 3], *p["ffn"][0], tail_a, ts=ts_ffn)
    q, k, v, c_end = _fox_proj(h2, p["g"][1, 0], p["kv_norm"], p["w_q"], p["w_k"], p["w_v"],
                               p["w_f2"], p["fg_b"], c0, ts=ts)
    o = _fox_attn(q, k, v, meta_kv, tq=tq)
    h4, tail_b_out = _ffn_layer(h2, p["g"][1, 2], p["g"][1, 3], *p["ffn"][1], tail_b, ts=ts_ffn,
                                attn=(o, p["b_w_out"], p["g"][1, 1]))
    return h4, (s_fin[0], tail_a_out[0], tail_b_out[0], c_end[0], (k[:1], v[:1]))


def kernel(x, meta_tokens, norm_gains, a_w_in, a_lb_logits, a_head_norm, a_w_out, kv_norm, kv_w, fg_b,
           b_w_q, b_w_out, ffn_w_up, ffn_conv, ffn_w_down):
    d = D_MODEL
    assert x.shape[2] == d and x.shape[1] % SEQ_TILE == 0 and meta_tokens.shape == (N_META, d)
    assert norm_gains.shape[0] == 2 and a_w_in.shape[0] == 1 and b_w_q.shape[0] == 1

    w_f = jnp.pad(kv_w[:, 2 * d:], ((0, 0), (0, LANES - B_HEADS))).astype(F32)
    w_f_hi = w_f.astype(BF16)
    params = {
        "g": norm_gains.astype(F32),
        "lb_logits": a_lb_logits.astype(F32),
        "head_gain": a_head_norm[0].astype(F32),
        "w_in": a_w_in[0].astype(BF16),
        "a_w_out": a_w_out[0].astype(BF16),
        "kv_norm": kv_norm.astype(F32),
        "w_q": b_w_q[0].astype(BF16),
        "w_k": kv_w[:, :d].astype(BF16),
        "w_v": kv_w[:, d:2 * d].astype(BF16),
        "w_f2": jnp.concatenate([w_f_hi, (w_f - w_f_hi.astype(F32)).astype(BF16)], axis=1),
        "fg_b": jnp.pad(fg_b.astype(F32), (0, LANES - B_HEADS)).reshape(1, LANES),
        "b_w_out": b_w_out[0].astype(BF16),
        "ffn": [_prep_ffn(ffn_w_up[l], ffn_conv[l], ffn_w_down[l]) for l in range(2)],
    }

    zero_carry = (jnp.zeros((A_HEADS, A_DK, A_DK), F32),
                  jnp.zeros((SUBLANES, 2 * D_FF_PAD), F32), jnp.zeros((SUBLANES, 2 * D_FF_PAD), F32),
                  jnp.zeros((1, LANES), F32), None)
    _, meta_carry = _trunk(meta_tokens[None].astype(F32), params, zero_carry,
                           ts=N_META, ts_ffn=N_META, chunk=N_META, tq=N_META)
    out, _ = _trunk(x, params, meta_carry, ts=SEQ_TILE, ts_ffn=FFN_TILE, chunk=A_CHUNK, tq=Q_TILE)
    return out
```

```python
import functools

import numpy as np
import jax
import jax.numpy as jnp
from jax import lax
from jax.experimental import pallas as pl
from jax.experimental.pallas import tpu as pltpu

F32 = jnp.float32
BF16 = jnp.bfloat16

D_MODEL = 1024
N_META = 16
A_HEADS = 8
A_DK = D_MODEL // A_HEADS
A_CHUNK = 64
B_HEADS = 16
B_HDIM = D_MODEL // B_HEADS
D_FF = 2752
EPS = 1e-6

LANES = 128
SUBLANES = 8
MXU_DIM = 256
D_FF_PAD = -(-D_FF // MXU_DIM) * MXU_DIM
ATT_PAD = LANES
AUX_LANES = LANES // B_HEADS
VMEM_LIMIT = 56 * 1024 * 1024

SEQ_TILE = 512
GLA_TILE = 256
FFN_TILE = 512
Q_TILE = 256
PROJ_COLS = MXU_DIM
GATE_COLS = MXU_DIM
CONV_ROWS = 64
CONV_COLS = 256
HEADS_PER_STEP = 4
ATT_STRIP = 16
MASKED = -1e30
LOG2E = 1.4426950408889634


def _rms(x, g):
    return x * lax.rsqrt(jnp.mean(x * x, axis=-1, keepdims=True) + EPS) * g


def _sigmoid(x):
    return 1.0 / (1.0 + jnp.exp(-x))


def _split3(x):
    hi = x.astype(BF16)
    r = x - hi.astype(F32)
    mid = r.astype(BF16)
    lo = (r - mid.astype(F32)).astype(BF16)
    return hi, mid, lo


def _tril(n):
    row = lax.broadcasted_iota(jnp.int32, (n, n), 0)
    col = lax.broadcasted_iota(jnp.int32, (n, n), 1)
    return row >= col


def _cumsum_rows(x, tril_bf16):
    hi, mid, lo = _split3(x)
    dot = functools.partial(jnp.dot, preferred_element_type=F32)
    return dot(tril_bf16, hi) + dot(tril_bf16, mid) + dot(tril_bf16, lo)


def _dot_nt(a, b):
    return lax.dot_general(a, b, (((1,), (1,)), ((), ())), preferred_element_type=F32)


def _const_spec(shape):
    nd = len(shape)
    return pl.BlockSpec(shape, lambda *_: (0,) * nd, pipeline_mode=pl.Buffered(1))


def _params():
    return pltpu.CompilerParams(dimension_semantics=("arbitrary", "arbitrary"),
                                vmem_limit_bytes=VMEM_LIMIT)


def _hgrn2_kernel(x_ref, gpre_ref, gpost_ref, lbl_ref, hg_ref, win_ref, wout_ref, s0_ref,
                  out_ref, sfin_ref, proj_s, qin_s, kin_s, kout_s, v_s, dec_s, y_s, st_s, *, ts, chunk, layer):
    t = pl.program_id(1)
    n_chunks = ts // chunk

    @pl.when(t == 0)
    def _():
        st_s[...] = s0_ref[...]

    lbl = lbl_ref[...]
    e = jnp.exp(lbl - jnp.max(lbl, axis=0, keepdims=True))
    lb = jnp.sum(e[:layer + 1], axis=0, keepdims=True) / jnp.sum(e, axis=0, keepdims=True)

    xn = _rms(x_ref[0], gpre_ref[...]).astype(BF16)
    proj_s[...] = jnp.dot(xn, win_ref[...], preferred_element_type=F32)

    scan_row = lax.broadcasted_iota(jnp.int32, (chunk, GATE_COLS), 0)
    for c in range(n_chunks):
        rows = slice(c * chunk, (c + 1) * chunk)
        for c0 in range(0, D_MODEL, GATE_COLS):
            cols = slice(c0, c0 + GATE_COLS)
            lbc = lb[:, cols]
            f = lbc + (1.0 - lbc) * _sigmoid(proj_s[rows, D_MODEL + c0:D_MODEL + c0 + GATE_COLS])
            k = 1.0 - f
            b = jnp.log(f)
            for s in (1, 2, 4, 8, 16, 32):
                if s < chunk:
                    b = b + jnp.where(scan_row >= s, pltpu.roll(b, s, 0), 0.0)
            dec = jnp.exp(b[chunk - 1:chunk, :])
            k_in = k * jnp.exp(-b)
            qin_s[rows, cols] = (proj_s[rows, cols] * jnp.exp(b)).astype(BF16)
            kin_s[rows, cols] = k_in.astype(BF16)
            kout_s[rows, cols] = (k_in * dec).astype(BF16)
            v_s[rows, cols] = proj_s[rows, 2 * D_MODEL + c0:2 * D_MODEL + c0 + GATE_COLS].astype(BF16)
            dec_s[c:c + 1, cols] = dec

    gt = min(ts, GLA_TILE)
    sub_chunks = gt // chunk
    row_c = lax.broadcasted_iota(jnp.int32, (gt, gt), 0)
    col_c = lax.broadcasted_iota(jnp.int32, (gt, gt), 1)
    intra = (row_c >= col_c) & (row_c // chunk == col_c // chunk)
    vt_chunk = lax.broadcasted_iota(jnp.int32, (A_DK, gt), 1) // chunk
    hgain = hg_ref[...]
    for r0 in range(0, ts, gt):
        rows = slice(r0, r0 + gt)
        for h in range(A_HEADS):
            cols = slice(h * A_DK, (h + 1) * A_DK)
            q_in = qin_s[rows, cols]
            attn = jnp.where(intra, _dot_nt(q_in, kin_s[rows, cols]), 0.0).astype(BF16)
            o_intra = jnp.dot(attn, v_s[rows, cols], preferred_element_type=F32)
            v_t = proj_s[rows, 2 * D_MODEL + h * A_DK:2 * D_MODEL + (h + 1) * A_DK].T.astype(BF16)
            v_t_blocks = jnp.concatenate([jnp.where(vt_chunk == c, v_t, jnp.zeros_like(v_t))
                                          for c in range(sub_chunks)], axis=0)
            d_st = jnp.dot(v_t_blocks, kout_s[rows, cols], preferred_element_type=F32)
            st = st_s[h]
            o_inter = []
            for c in range(sub_chunks):
                o_inter.append(_dot_nt(q_in[c * chunk:(c + 1) * chunk], st.astype(BF16)))
                dec = dec_s[r0 // chunk + c:r0 // chunk + c + 1, cols]
                st = st * dec + d_st[c * A_DK:(c + 1) * A_DK]
            st_s[h] = st
            o = o_intra + jnp.concatenate(o_inter, axis=0)
            o = o * lax.rsqrt(jnp.mean(o * o, axis=-1, keepdims=True) + EPS)
            gg = proj_s[rows, 3 * D_MODEL + h * A_DK:3 * D_MODEL + (h + 1) * A_DK]
            y_s[rows, cols] = (o * hgain[:, cols] * (gg * _sigmoid(gg))).astype(BF16)

    mix = jnp.dot(y_s[...], wout_ref[...], preferred_element_type=F32)
    out_ref[0] = x_ref[0] + _rms(mix, gpost_ref[...])

    @pl.when(t == pl.num_programs(1) - 1)
    def _():
        sfin_ref[0] = st_s[...]


def _hgrn2_layer(h, g_pre, g_post, lb_logits, head_gain, w_in, w_out, s0, *, ts, chunk, layer):
    bsz, t_len, d = h.shape
    assert t_len % ts == 0 and ts % chunk == 0
    row = lambda v: v.reshape(1, d)
    st_shape = (A_HEADS, A_DK, A_DK)
    out, s_fin = pl.pallas_call(
        functools.partial(_hgrn2_kernel, ts=ts, chunk=chunk, layer=layer),
        grid=(bsz, t_len // ts),
        in_specs=[
            pl.BlockSpec((1, ts, d), lambda b, t: (b, t, 0)),
            _const_spec((1, d)), _const_spec((1, d)), _const_spec(lb_logits.shape), _const_spec((1, d)),
            _const_spec(w_in.shape), _const_spec(w_out.shape), _const_spec(st_shape),
        ],
        out_specs=[
            pl.BlockSpec((1, ts, d), lambda b, t: (b, t, 0)),
            pl.BlockSpec((1,) + st_shape, lambda b, t: (b, 0, 0, 0)),
        ],
        out_shape=[jax.ShapeDtypeStruct(h.shape, F32),
                   jax.ShapeDtypeStruct((bsz,) + st_shape, F32)],
        scratch_shapes=[pltpu.VMEM((ts, 4 * d), F32)] + [pltpu.VMEM((ts, d), BF16)] * 4
                       + [pltpu.VMEM((max(ts // chunk, SUBLANES), d), F32), pltpu.VMEM((ts, d), BF16),
                          pltpu.VMEM(st_shape, F32)],
        compiler_params=_params(),
        name="hgrn2_mixer",
    )(h, row(g_pre), row(g_post), lb_logits, row(head_gain), w_in, w_out, s0)
    return out, s_fin


def _ffn_kernel(*refs, ts, with_attn):
    if with_attn:
        (h_ref, o_ref, wo_ref, gmix_ref, g2_ref, g3_ref, wup_ref, conv_ref, wdown_ref, tail0_ref,
         out_ref, tail_ref, u_s, act_s) = refs
    else:
        (h_ref, g2_ref, g3_ref, wup_ref, conv_ref, wdown_ref, tail0_ref,
         out_ref, tail_ref, u_s, act_s) = refs
    t = pl.program_id(1)
    halo = SUBLANES

    @pl.when(t == 0)
    def _():
        u_s[0:halo, :] = tail0_ref[...]

    h = h_ref[0]
    if with_attn:
        mix = jnp.dot(o_ref[0], wo_ref[...], preferred_element_type=F32)
        h = h + _rms(mix, gmix_ref[...])
    xn = _rms(h, g2_ref[...]).astype(BF16)
    rb = min(CONV_ROWS, ts)
    for j in range(D_FF_PAD // CONV_COLS):
        starts = (j * CONV_COLS, D_FF_PAD + j * CONV_COLS)
        for c0 in starts:
            u_s[halo:halo + ts, c0:c0 + CONV_COLS] = jnp.dot(
                xn, wup_ref[:, c0:c0 + CONV_COLS], preferred_element_type=F32)
        for r0 in range(0, ts, rb):
            def conv(c0):
                w = conv_ref[:, c0:c0 + CONV_COLS]
                blk = u_s[r0:r0 + rb + halo, c0:c0 + CONV_COLS]
                tap = lambda d: (pltpu.roll(blk, d, 0) if d else blk)[halo:halo + rb]
                return w[0:1] * tap(2) + w[1:2] * tap(1) + w[2:3] * tap(0)
            cg, cv = conv(starts[0]), conv(starts[1])
            act_s[r0:r0 + rb, j * CONV_COLS:(j + 1) * CONV_COLS] = (cg * _sigmoid(cg) * cv).astype(BF16)
    u_s[0:halo, :] = u_s[ts:ts + halo, :]

    ff = jnp.dot(act_s[...], wdown_ref[...], preferred_element_type=F32)
    out_ref[0] = h + _rms(ff, g3_ref[...])

    @pl.when(t == pl.num_programs(1) - 1)
    def _():
        tail_ref[0] = u_s[0:halo, :]


def _ffn_layer(h, g2, g3, w_up, conv_w, w_down, tail0, *, ts, attn=None):
    bsz, t_len, d = h.shape
    assert t_len % ts == 0
    row = lambda v: v.reshape(1, d)
    tile = pl.BlockSpec((1, ts, d), lambda b, t: (b, t, 0))
    args, specs = [h], [tile]
    if attn is not None:
        o, w_o, g_mix = attn
        args += [o, w_o, row(g_mix)]
        specs += [tile, _const_spec(w_o.shape), _const_spec((1, d))]
    args += [row(g2), row(g3), w_up, conv_w, w_down, tail0]
    specs += [_const_spec((1, d)), _const_spec((1, d)), _const_spec(w_up.shape), _const_spec(conv_w.shape),
              _const_spec(w_down.shape), _const_spec(tail0.shape)]
    out, tail = pl.pallas_call(
        functools.partial(_ffn_kernel, ts=ts, with_attn=attn is not None),
        grid=(bsz, t_len // ts),
        in_specs=specs,
        out_specs=[tile, pl.BlockSpec((1, SUBLANES, 2 * D_FF_PAD), lambda b, t: (b, 0, 0))],
        out_shape=[jax.ShapeDtypeStruct(h.shape, F32),
                   jax.ShapeDtypeStruct((bsz, SUBLANES, 2 * D_FF_PAD), F32)],
        scratch_shapes=[pltpu.VMEM((ts + SUBLANES, 2 * D_FF_PAD), F32), pltpu.VMEM((ts, D_FF_PAD), BF16)],
        compiler_params=_params(),
        name="conv_ffn",
    )(*args)
    return out, tail


def _fox_proj_kernel(h_ref, gq_ref, gkv_ref, wq_ref, wk_ref, wv_ref, wf2_ref, fgb_ref, sel_ref,
                     c0_ref, q_out, k_out, v_out, cend_ref, carry_s, *, ts):
    t = pl.program_id(1)

    @pl.when(t == 0)
    def _():
        carry_s[...] = c0_ref[...]

    h = h_ref[0]
    y = h * lax.rsqrt(jnp.mean(h * h, axis=-1, keepdims=True) + EPS)
    xq = (y * gq_ref[...]).astype(BF16)
    xkv = y * gkv_ref[...]
    xkv_hi = xkv.astype(BF16)
    xkv_lo = (xkv - xkv_hi.astype(F32)).astype(BF16)
    dot = functools.partial(jnp.dot, preferred_element_type=F32)
    zf2 = dot(xkv_hi, wf2_ref[...])
    zf = (zf2[:, :LANES] + zf2[:, LANES:] + dot(xkv_lo, wf2_ref[:, :LANES])
          + fgb_ref[...])
    ls = jnp.minimum(zf, 0.0) - jnp.log(1.0 + jnp.exp(-jnp.abs(zf)))
    c = carry_s[...] + _cumsum_rows(ls, _tril(ts).astype(BF16))
    carry_s[...] = c[ts - 1:ts, :]

    c2 = c * np.float32(LOG2E)
    c_hi = c2.astype(BF16).astype(F32)
    r = c2 - c_hi
    c_mid = r.astype(BF16).astype(F32)
    c_lo = (r - c_mid).astype(BF16).astype(F32)
    lane = lax.broadcasted_iota(jnp.int32, (ts, LANES), 1)
    parts = jnp.where(lane < B_HEADS, c_hi,
                      jnp.where(lane < 2 * B_HEADS, pltpu.roll(c_mid, B_HEADS, 1),
                                jnp.where(lane < 3 * B_HEADS, pltpu.roll(c_lo, 2 * B_HEADS, 1), 0.0)))
    aux = dot(parts.astype(BF16), sel_ref[...])
    data = lane < B_HDIM
    plus = lane < B_HDIM + 3
    minus = lane < B_HDIM + 6
    heads_per_block = PROJ_COLS // B_HDIM
    for c0 in range(0, D_MODEL, PROJ_COLS):
        cols = slice(c0, c0 + PROJ_COLS)
        qb = dot(xq, wq_ref[:, cols]) * np.float32(LOG2E / np.sqrt(B_HDIM))
        kb = dot(xkv_hi, wk_ref[:, cols])
        vb = dot(xkv_hi, wv_ref[:, cols])
        hd0 = c0 // B_HDIM
        for j in range(heads_per_block):
            def head(x):
                x = x[:, (j // 2) * LANES:(j // 2 + 1) * LANES]
                return pltpu.roll(x, B_HDIM, 1) if j % 2 else x
            hd = hd0 + j
            shift = (B_HDIM - AUX_LANES * hd) % LANES
            a = pltpu.roll(aux, shift, 1) if shift else aux
            q_out[0, hd] = jnp.where(data, head(qb), jnp.where(plus, a, jnp.where(minus, 1.0, 0.0))).astype(BF16)
            k_out[0, hd] = jnp.where(data, head(kb), jnp.where(plus, 1.0, jnp.where(minus, a, 0.0))).astype(BF16)
            v_out[0, hd] = jnp.where(data, head(vb), jnp.where(lane == B_HDIM, 1.0, 0.0)).astype(BF16)

    @pl.when(t == pl.num_programs(1) - 1)
    def _():
        cend_ref[0] = carry_s[...]


def _fox_sel_matrix():
    sel = np.zeros((LANES, LANES), np.float32)
    for term in range(3):
        for hd in range(B_HEADS):
            sel[term * B_HEADS + hd, AUX_LANES * hd + term] = 1.0
            sel[term * B_HEADS + hd, AUX_LANES * hd + 3 + term] = -1.0
    return jnp.asarray(sel, BF16)


def _fox_proj(h, g_q, g_kv, w_q, w_k, w_v, w_f2, fg_b, c0, *, ts):
    bsz, t_len, d = h.shape
    assert t_len % ts == 0
    row = lambda v: v.reshape(1, d)
    sel = _fox_sel_matrix()
    head_shape = (bsz, B_HEADS, t_len, ATT_PAD)
    head_spec = pl.BlockSpec((1, B_HEADS, ts, ATT_PAD), lambda b, t: (b, 0, t, 0))
    return pl.pallas_call(
        functools.partial(_fox_proj_kernel, ts=ts),
        grid=(bsz, t_len // ts),
        in_specs=[
            pl.BlockSpec((1, ts, d), lambda b, t: (b, t, 0)),
            _const_spec((1, d)), _const_spec((1, d)),
            _const_spec(w_q.shape), _const_spec(w_k.shape), _const_spec(w_v.shape),
            _const_spec(w_f2.shape), _const_spec(fg_b.shape),
            _const_spec(sel.shape), _const_spec(c0.shape),
        ],
        out_specs=[head_spec, head_spec, head_spec,
                   pl.BlockSpec((1, 1, LANES), lambda b, t: (b, 0, 0))],
        out_shape=[jax.ShapeDtypeStruct(head_shape, BF16)] * 3 + [jax.ShapeDtypeStruct((bsz, 1, LANES), F32)],
        scratch_shapes=[pltpu.VMEM((1, LANES), F32)],
        compiler_params=_params(),
        name="fox_proj",
    )(h, row(g_q), row(g_kv), w_q, w_k, w_v, w_f2, fg_b, sel, c0)


def _fox_attn_kernel(*refs, tq, hp, n_meta):
    if n_meta:
        q_ref, k_ref, v_ref, km_ref, vm_ref, o_ref, kcat_s, vcat_s, s_a, s_b, p_a, p_b = refs
    else:
        q_ref, k_ref, v_ref, o_ref, kcat_s, vcat_s, s_a, s_b, p_a, p_b = refs
    t_len = q_ref.shape[2]
    n_keys = n_meta + t_len
    w_max = kcat_s.shape[1]
    for hh in range(hp):
        if n_meta:
            kcat_s[hh, 0:n_meta] = km_ref[0, hh]
            vcat_s[hh, 0:n_meta] = vm_ref[0, hh]
        kcat_s[hh, n_meta:n_keys] = k_ref[0, hh]
        vcat_s[hh, n_meta:n_keys] = v_ref[0, hh]
        if w_max > n_keys:
            vcat_s[hh, n_keys:w_max] = jnp.zeros((w_max - n_keys, vcat_s.shape[2]), BF16)

    units = [(i, hh) for i in range(-(-n_keys // tq)) for hh in range(hp)]

    def geometry(i):
        k0 = i * tq
        a, e = max(n_meta, k0), min(k0 + tq, n_keys)
        return k0, a, e, e - a, -(-e // LANES) * LANES

    def logit_pieces(u):
        (i, hh), s_s = units[u], (s_a, s_b)[u % 2]
        k0, a, e, nr, w = geometry(i)
        diag_ok = (lax.broadcasted_iota(jnp.int32, (nr, e - k0), 1)
                   <= lax.broadcasted_iota(jnp.int32, (nr, e - k0), 0) + (a - k0))

        def piece(kt):
            def run():
                q = q_ref[0, hh, a - n_meta:e - n_meta, :]
                if kt < k0:
                    s_s[0:nr, kt:kt + tq] = _dot_nt(q, kcat_s[hh, kt:kt + tq])
                else:
                    s_s[0:nr, k0:e] = jnp.where(diag_ok, _dot_nt(q, kcat_s[hh, k0:e]), MASKED)
                    if w > e:
                        s_s[0:nr, e:w] = jnp.full((nr, w - e), MASKED, F32)
            return run
        return [piece(kt) for kt in range(0, e, tq)]

    def softmax_pieces(u):
        (i, hh), s_s, p_s = units[u], (s_a, s_b)[u % 2], (p_a, p_b)[u % 2]
        k0, a, e, nr, w = geometry(i)
        rs = min(nr, ATT_STRIP)

        def piece(r):
            def run():
                x = s_s[r:r + rs, 0:w]
                m = jnp.max(x, axis=-1, keepdims=True)
                p_s[r:r + rs, 0:w] = jnp.exp2(x - m).astype(BF16)
            return run
        return [piece(r) for r in range(0, nr, rs)]

    def value_pieces(u):
        (i, hh), p_s = units[u], (p_a, p_b)[u % 2]
        k0, a, e, nr, w = geometry(i)

        def run():
            acc = jnp.dot(p_s[0:nr, 0:w], vcat_s[hh, 0:w], preferred_element_type=F32)
            o = acc[:, :B_HDIM] / acc[:, B_HDIM:B_HDIM + 1]
            o_ref[0, a - n_meta:e - n_meta, hh * B_HDIM:(hh + 1) * B_HDIM] = o.astype(BF16)
        return [run]

    for step in range(len(units) + 2):
        streams = [f(u) for f, u in ((value_pieces, step - 2), (softmax_pieces, step - 1), (logit_pieces, step))
                   if 0 <= u < len(units)]
        keyed = [((j + 0.5) / len(st), n, run) for n, st in enumerate(streams) for j, run in enumerate(st)]
        for _, _, run in sorted(keyed, key=lambda t: t[:2]):
            run()


def _fox_attn(q, k, v, meta_kv, *, tq, hp=HEADS_PER_STEP):
    bsz, heads, t_len, pad = q.shape
    assert heads % hp == 0
    head_spec = pl.BlockSpec((1, hp, t_len, pad), lambda b, g: (b, g, 0, 0))
    args, specs = [q, k, v], [head_spec] * 3
    n_meta = 0
    if meta_kv is not None:
        n_meta = meta_kv[0].shape[2]
        args += list(meta_kv)
        specs += [pl.BlockSpec((1, hp, n_meta, pad), lambda b, g: (0, g, 0, 0))] * 2
    w_max = -(-(n_meta + t_len) // LANES) * LANES
    return pl.pallas_call(
        functools.partial(_fox_attn_kernel, tq=tq, hp=hp, n_meta=n_meta),
        grid=(bsz, heads // hp),
        in_specs=specs,
        out_specs=pl.BlockSpec((1, t_len, hp * B_HDIM), lambda b, g: (b, 0, g)),
        out_shape=jax.ShapeDtypeStruct((bsz, t_len, heads * B_HDIM), BF16),
        scratch_shapes=[pltpu.VMEM((hp, w_max, pad), BF16)] * 2
                       + [pltpu.VMEM((tq, w_max), F32)] * 2 + [pltpu.VMEM((tq, w_max), BF16)] * 2,
        compiler_params=_params(),
        name="fox_attention",
    )(*args)


def _pad_cols(w, n):
    return jnp.pad(w, ((0, 0), (0, n - w.shape[1])))


def _prep_ffn(w_up, conv_w, w_down):
    up = jnp.concatenate([_pad_cols(w_up[:, :D_FF], D_FF_PAD), _pad_cols(w_up[:, D_FF:], D_FF_PAD)], axis=1)
    cw = jnp.concatenate([_pad_cols(conv_w[:, :D_FF], D_FF_PAD), _pad_cols(conv_w[:, D_FF:], D_FF_PAD)], axis=1)
    down = jnp.pad(w_down, ((0, D_FF_PAD - D_FF), (0, 0)))
    return up.astype(BF16), cw.astype(F32), down.astype(BF16)


def _trunk(h, params, carry_in, *, ts, ts_ffn, chunk, tq):
    s0, tail_a, tail_b, c0, meta_kv = carry_in
    p = params
    h1, s_fin = _hgrn2_layer(h, p["g"][0, 0], p["g"][0, 1], p["lb_logits"], p["head_gain"], p["w_in"],
                             p["a_w_out"], s0, ts=ts, chunk=chunk, layer=0)
    h2, tail_a_out = _ffn_layer(h1, p["g"][0, 2], p["g"][0, 3], *p["ffn"][0], tail_a, ts=ts_ffn)
    q, k, v, c_end = _fox_proj(h2, p["g"][1, 0], p["kv_norm"], p["w_q"], p["w_k"], p["w_v"],
                               p["w_f2"], p["fg_b"], c0, ts=ts)
    o = _fox_attn(q, k, v, meta_kv, tq=tq)
    h4, tail_b_out = _ffn_layer(h2, p["g"][1, 2], p["g"][1, 3], *p["ffn"][1], tail_b, ts=ts_ffn,
                                attn=(o, p["b_w_out"], p["g"][1, 1]))
    return h4, (s_fin[0], tail_a_out[0], tail_b_out[0], c_end[0], (k[:1], v[:1]))


def kernel(x, meta_tokens, norm_gains, a_w_in, a_lb_logits, a_head_norm, a_w_out, kv_norm, kv_w, fg_b,
           b_w_q, b_w_out, ffn_w_up, ffn_conv, ffn_w_down):
    d = D_MODEL
    assert x.shape[2] == d and x.shape[1] % SEQ_TILE == 0 and meta_tokens.shape == (N_META, d)
    assert norm_gains.shape[0] == 2 and a_w_in.shape[0] == 1 and b_w_q.shape[0] == 1

    w_f = jnp.pad(kv_w[:, 2 * d:], ((0, 0), (0, LANES - B_HEADS))).astype(F32)
    w_f_hi = w_f.astype(BF16)
    params = {
        "g": norm_gains.astype(F32),
        "lb_logits": a_lb_logits.astype(F32),
        "head_gain": a_head_norm[0].astype(F32),
        "w_in": a_w_in[0].astype(BF16),
        "a_w_out": a_w_out[0].astype(BF16),
        "kv_norm": kv_norm.astype(F32),
        "w_q": b_w_q[0].astype(BF16),
        "w_k": kv_w[:, :d].astype(BF16),
        "w_v": kv_w[:, d:2 * d].astype(BF16),
        "w_f2": jnp.concatenate([w_f_hi, (w_f - w_f_hi.astype(F32)).astype(BF16)], axis=1),
        "fg_b": jnp.pad(fg_b.astype(F32), (0, LANES - B_HEADS)).reshape(1, LANES),
        "b_w_out": b_w_out[0].astype(BF16),
        "ffn": [_prep_ffn(ffn_w_up[l], ffn_conv[l], ffn_w_down[l]) for l in range(2)],
    }

    zero_carry = (jnp.zeros((A_HEADS, A_DK, A_DK), F32),
                  jnp.zeros((SUBLANES, 2 * D_FF_PAD), F32), jnp.zeros((SUBLANES, 2 * D_FF_PAD), F32),
                  jnp.zeros((1, LANES), F32), None)
    _, meta_carry = _trunk(meta_tokens[None].astype(F32), params, zero_carry,
                           ts=N_META, ts_ffn=N_META, chunk=N_META, tq=N_META)
    out, _ = _trunk(x, params, meta_carry, ts=SEQ_TILE, ts_ffn=FFN_TILE, chunk=A_CHUNK, tq=Q_TILE)
    return out
```

```python
import functools

import numpy as np
import jax
import jax.numpy as jnp
from jax import lax
from jax.experimental import pallas as pl
from jax.experimental.pallas import tpu as pltpu

F32 = jnp.float32
BF16 = jnp.bfloat16

D_MODEL = 1024
N_META = 16
A_HEADS = 8
A_DK = D_MODEL // A_HEADS
A_CHUNK = 64
B_HEADS = 16
B_HDIM = D_MODEL // B_HEADS
D_FF = 2752
EPS = 1e-6

LANES = 128
SUBLANES = 8
MXU_DIM = 256
D_FF_PAD = -(-D_FF // MXU_DIM) * MXU_DIM
ATT_PAD = LANES
AUX_LANES = LANES // B_HEADS
VMEM_LIMIT = 56 * 1024 * 1024

SEQ_TILE = 512
GLA_TILE = 256
FFN_TILE = 512
Q_TILE = 256
PROJ_COLS = MXU_DIM
GATE_COLS = MXU_DIM
CONV_ROWS = 64
CONV_COLS = 256
HEADS_PER_STEP = 4
ATT_STRIP = 16
MASKED = -1e30
LOG2E = 1.4426950408889634


def _rms(x, g):
    return x * lax.rsqrt(jnp.mean(x * x, axis=-1, keepdims=True) + EPS) * g


def _sigmoid(x):
    return 1.0 / (1.0 + jnp.exp(-x))


def _split3(x):
    hi = x.astype(BF16)
    r = x - hi.astype(F32)
    mid = r.astype(BF16)
    lo = (r - mid.astype(F32)).astype(BF16)
    return hi, mid, lo


def _tril(n):
    row = lax.broadcasted_iota(jnp.int32, (n, n), 0)
    col = lax.broadcasted_iota(jnp.int32, (n, n), 1)
    return row >= col


def _cumsum_rows(x, tril_bf16):
    hi, mid, lo = _split3(x)
    dot = functools.partial(jnp.dot, preferred_element_type=F32)
    return dot(tril_bf16, hi) + dot(tril_bf16, mid) + dot(tril_bf16, lo)


def _dot_nt(a, b):
    return lax.dot_general(a, b, (((1,), (1,)), ((), ())), preferred_element_type=F32)


def _const_spec(shape):
    nd = len(shape)
    return pl.BlockSpec(shape, lambda *_: (0,) * nd, pipeline_mode=pl.Buffered(1))


def _params():
    return pltpu.CompilerParams(dimension_semantics=("arbitrary", "arbitrary"),
                                vmem_limit_bytes=VMEM_LIMIT)


def _hgrn2_kernel(x_ref, gpre_ref, gpost_ref, lbl_ref, hg_ref, win_ref, wout_ref, s0_ref,
                  out_ref, sfin_ref, proj_s, qin_s, kin_s, kout_s, v_s, dec_s, y_s, st_s, *, ts, chunk, layer):
    t = pl.program_id(1)
    n_chunks = ts // chunk

    @pl.when(t == 0)
    def _():
        st_s[...] = s0_ref[...]

    lbl = lbl_ref[...]
    e = jnp.exp(lbl - jnp.max(lbl, axis=0, keepdims=True))
    lb = jnp.sum(e[:layer + 1], axis=0, keepdims=True) / jnp.sum(e, axis=0, keepdims=True)

    xn = _rms(x_ref[0], gpre_ref[...]).astype(BF16)
    proj_s[...] = jnp.dot(xn, win_ref[...], preferred_element_type=F32)

    tril3 = jnp.concatenate([_tril(chunk).astype(BF16)] * 3, axis=1)
    for c in range(n_chunks):
        rows = slice(c * chunk, (c + 1) * chunk)
        for c0 in range(0, D_MODEL, GATE_COLS):
            cols = slice(c0, c0 + GATE_COLS)
            lbc = lb[:, cols]
            f = lbc + (1.0 - lbc) * _sigmoid(proj_s[rows, D_MODEL + c0:D_MODEL + c0 + GATE_COLS])
            k = 1.0 - f
            b = jnp.dot(tril3, jnp.concatenate(_split3(jnp.log(f)), axis=0), preferred_element_type=F32)
            dec = jnp.exp(b[chunk - 1:chunk, :])
            k_in = k * jnp.exp(-b)
            qin_s[rows, cols] = (proj_s[rows, cols] * jnp.exp(b)).astype(BF16)
            kin_s[rows, cols] = k_in.astype(BF16)
            kout_s[rows, cols] = (k_in * dec).astype(BF16)
            v_s[rows, cols] = proj_s[rows, 2 * D_MODEL + c0:2 * D_MODEL + c0 + GATE_COLS].astype(BF16)
            dec_s[c:c + 1, cols] = dec

    gt = min(ts, GLA_TILE)
    sub_chunks = gt // chunk
    row_c = lax.broadcasted_iota(jnp.int32, (gt, gt), 0)
    col_c = lax.broadcasted_iota(jnp.int32, (gt, gt), 1)
    intra = (row_c >= col_c) & (row_c // chunk == col_c // chunk)
    vt_chunk = lax.broadcasted_iota(jnp.int32, (A_DK, gt), 1) // chunk
    hgain = hg_ref[...]
    for r0 in range(0, ts, gt):
        rows = slice(r0, r0 + gt)
        for h in range(A_HEADS):
            cols = slice(h * A_DK, (h + 1) * A_DK)
            q_in = qin_s[rows, cols]
            attn = jnp.where(intra, _dot_nt(q_in, kin_s[rows, cols]), 0.0).astype(BF16)
            o_intra = jnp.dot(attn, v_s[rows, cols], preferred_element_type=F32)
            v_t = proj_s[rows, 2 * D_MODEL + h * A_DK:2 * D_MODEL + (h + 1) * A_DK].T.astype(BF16)
            v_t_blocks = jnp.concatenate([jnp.where(vt_chunk == c, v_t, jnp.zeros_like(v_t))
                                          for c in range(sub_chunks)], axis=0)
            d_st = jnp.dot(v_t_blocks, kout_s[rows, cols], preferred_element_type=F32)
            st = st_s[h]
            o_inter = []
            for c in range(sub_chunks):
                o_inter.append(_dot_nt(q_in[c * chunk:(c + 1) * chunk], st.astype(BF16)))
                dec = dec_s[r0 // chunk + c:r0 // chunk + c + 1, cols]
                st = st * dec + d_st[c * A_DK:(c + 1) * A_DK]
            st_s[h] = st
            o = o_intra + jnp.concatenate(o_inter, axis=0)
            o = o * lax.rsqrt(jnp.mean(o * o, axis=-1, keepdims=True) + EPS)
            gg = proj_s[rows, 3 * D_MODEL + h * A_DK:3 * D_MODEL + (h + 1) * A_DK]
            y_s[rows, cols] = (o * hgain[:, cols] * (gg * _sigmoid(gg))).astype(BF16)

    mix = jnp.dot(y_s[...], wout_ref[...], preferred_element_type=F32)
    out_ref[0] = x_ref[0] + _rms(mix, gpost_ref[...])

    @pl.when(t == pl.num_programs(1) - 1)
    def _():
        sfin_ref[0] = st_s[...]


def _hgrn2_layer(h, g_pre, g_post, lb_logits, head_gain, w_in, w_out, s0, *, ts, chunk, layer):
    bsz, t_len, d = h.shape
    assert t_len % ts == 0 and ts % chunk == 0
    row = lambda v: v.reshape(1, d)
    st_shape = (A_HEADS, A_DK, A_DK)
    out, s_fin = pl.pallas_call(
        functools.partial(_hgrn2_kernel, ts=ts, chunk=chunk, layer=layer),
        grid=(bsz, t_len // ts),
        in_specs=[
            pl.BlockSpec((1, ts, d), lambda b, t: (b, t, 0)),
            _const_spec((1, d)), _const_spec((1, d)), _const_spec(lb_logits.shape), _const_spec((1, d)),
            _const_spec(w_in.shape), _const_spec(w_out.shape), _const_spec(st_shape),
        ],
        out_specs=[
            pl.BlockSpec((1, ts, d), lambda b, t: (b, t, 0)),
            pl.BlockSpec((1,) + st_shape, lambda b, t: (b, 0, 0, 0)),
        ],
        out_shape=[jax.ShapeDtypeStruct(h.shape, F32),
                   jax.ShapeDtypeStruct((bsz,) + st_shape, F32)],
        scratch_shapes=[pltpu.VMEM((ts, 4 * d), F32)] + [pltpu.VMEM((ts, d), BF16)] * 4
                       + [pltpu.VMEM((max(ts // chunk, SUBLANES), d), F32), pltpu.VMEM((ts, d), BF16),
                          pltpu.VMEM(st_shape, F32)],
        compiler_params=_params(),
        name="hgrn2_mixer",
    )(h, row(g_pre), row(g_post), lb_logits, row(head_gain), w_in, w_out, s0)
    return out, s_fin


def _ffn_kernel(*refs, ts, with_attn):
    if with_attn:
        (h_ref, o_ref, wo_ref, gmix_ref, g2_ref, g3_ref, wup_ref, conv_ref, wdown_ref, tail0_ref,
         out_ref, tail_ref, u_s, act_s) = refs
    else:
        (h_ref, g2_ref, g3_ref, wup_ref, conv_ref, wdown_ref, tail0_ref,
         out_ref, tail_ref, u_s, act_s) = refs
    t = pl.program_id(1)
    halo = SUBLANES

    @pl.when(t == 0)
    def _():
        u_s[0:halo, :] = tail0_ref[...]

    h = h_ref[0]
    if with_attn:
        mix = jnp.dot(o_ref[0], wo_ref[...], preferred_element_type=F32)
        h = h + _rms(mix, gmix_ref[...])
    xn = _rms(h, g2_ref[...]).astype(BF16)
    rb = min(CONV_ROWS, ts)
    for j in range(D_FF_PAD // CONV_COLS):
        starts = (j * CONV_COLS, D_FF_PAD + j * CONV_COLS)
        for c0 in starts:
            u_s[halo:halo + ts, c0:c0 + CONV_COLS] = jnp.dot(
                xn, wup_ref[:, c0:c0 + CONV_COLS], preferred_element_type=F32)
        for r0 in range(0, ts, rb):
            def conv(c0):
                w = conv_ref[:, c0:c0 + CONV_COLS]
                blk = u_s[r0:r0 + rb + halo, c0:c0 + CONV_COLS]
                tap = lambda d: (pltpu.roll(blk, d, 0) if d else blk)[halo:halo + rb]
                return w[0:1] * tap(2) + w[1:2] * tap(1) + w[2:3] * tap(0)
            cg, cv = conv(starts[0]), conv(starts[1])
            act_s[r0:r0 + rb, j * CONV_COLS:(j + 1) * CONV_COLS] = (cg * _sigmoid(cg) * cv).astype(BF16)
    u_s[0:halo, :] = u_s[ts:ts + halo, :]

    ff = jnp.dot(act_s[...], wdown_ref[...], preferred_element_type=F32)
    out_ref[0] = h + _rms(ff, g3_ref[...])

    @pl.when(t == pl.num_programs(1) - 1)
    def _():
        tail_ref[0] = u_s[0:halo, :]


def _ffn_layer(h, g2, g3, w_up, conv_w, w_down, tail0, *, ts, attn=None):
    bsz, t_len, d = h.shape
    assert t_len % ts == 0
    row = lambda v: v.reshape(1, d)
    tile = pl.BlockSpec((1, ts, d), lambda b, t: (b, t, 0))
    args, specs = [h], [tile]
    if attn is not None:
        o, w_o, g_mix = attn
        args += [o, w_o, row(g_mix)]
        specs += [tile, _const_spec(w_o.shape), _const_spec((1, d))]
    args += [row(g2), row(g3), w_up, conv_w, w_down, tail0]
    specs += [_const_spec((1, d)), _const_spec((1, d)), _const_spec(w_up.shape), _const_spec(conv_w.shape),
              _const_spec(w_down.shape), _const_spec(tail0.shape)]
    out, tail = pl.pallas_call(
        functools.partial(_ffn_kernel, ts=ts, with_attn=attn is not None),
        grid=(bsz, t_len // ts),
        in_specs=specs,
        out_specs=[tile, pl.BlockSpec((1, SUBLANES, 2 * D_FF_PAD), lambda b, t: (b, 0, 0))],
        out_shape=[jax.ShapeDtypeStruct(h.shape, F32),
                   jax.ShapeDtypeStruct((bsz, SUBLANES, 2 * D_FF_PAD), F32)],
        scratch_shapes=[pltpu.VMEM((ts + SUBLANES, 2 * D_FF_PAD), F32), pltpu.VMEM((ts, D_FF_PAD), BF16)],
        compiler_params=_params(),
        name="conv_ffn",
    )(*args)
    return out, tail


def _fox_proj_kernel(h_ref, gq_ref, gkv_ref, wq_ref, wk_ref, wv_ref, wf2_ref, fgb_ref, sel_ref,
                     c0_ref, q_out, k_out, v_out, cend_ref, carry_s, *, ts):
    t = pl.program_id(1)

    @pl.when(t == 0)
    def _():
        carry_s[...] = c0_ref[...]

    h = h_ref[0]
    y = h * lax.rsqrt(jnp.mean(h * h, axis=-1, keepdims=True) + EPS)
    xq = (y * gq_ref[...]).astype(BF16)
    xkv = y * gkv_ref[...]
    xkv_hi = xkv.astype(BF16)
    xkv_lo = (xkv - xkv_hi.astype(F32)).astype(BF16)
    dot = functools.partial(jnp.dot, preferred_element_type=F32)
    zf2 = dot(xkv_hi, wf2_ref[...])
    zf = (zf2[:, :LANES] + zf2[:, LANES:] + dot(xkv_lo, wf2_ref[:, :LANES])
          + fgb_ref[...])
    ls = jnp.minimum(zf, 0.0) - jnp.log(1.0 + jnp.exp(-jnp.abs(zf)))
    c = carry_s[...] + _cumsum_rows(ls, _tril(ts).astype(BF16))
    carry_s[...] = c[ts - 1:ts, :]

    c2 = c * np.float32(LOG2E)
    c_hi = c2.astype(BF16).astype(F32)
    r = c2 - c_hi
    c_mid = r.astype(BF16).astype(F32)
    c_lo = (r - c_mid).astype(BF16).astype(F32)
    lane = lax.broadcasted_iota(jnp.int32, (ts, LANES), 1)
    parts = jnp.where(lane < B_HEADS, c_hi,
                      jnp.where(lane < 2 * B_HEADS, pltpu.roll(c_mid, B_HEADS, 1),
                                jnp.where(lane < 3 * B_HEADS, pltpu.roll(c_lo, 2 * B_HEADS, 1), 0.0)))
    aux = dot(parts.astype(BF16), sel_ref[...])
    data = lane < B_HDIM
    plus = lane < B_HDIM + 3
    minus = lane < B_HDIM + 6
    heads_per_block = PROJ_COLS // B_HDIM
    for c0 in range(0, D_MODEL, PROJ_COLS):
        cols = slice(c0, c0 + PROJ_COLS)
        qb = dot(xq, wq_ref[:, cols]) * np.float32(LOG2E / np.sqrt(B_HDIM))
        kb = dot(xkv_hi, wk_ref[:, cols])
        vb = dot(xkv_hi, wv_ref[:, cols])
        hd0 = c0 // B_HDIM
        for j in range(heads_per_block):
            def head(x):
                x = x[:, (j // 2) * LANES:(j // 2 + 1) * LANES]
                return pltpu.roll(x, B_HDIM, 1) if j % 2 else x
            hd = hd0 + j
            shift = (B_HDIM - AUX_LANES * hd) % LANES
            a = pltpu.roll(aux, shift, 1) if shift else aux
            q_out[0, hd] = jnp.where(data, head(qb), jnp.where(plus, a, jnp.where(minus, 1.0, 0.0))).astype(BF16)
            k_out[0, hd] = jnp.where(data, head(kb), jnp.where(plus, 1.0, jnp.where(minus, a, 0.0))).astype(BF16)
            v_out[0, hd] = jnp.where(data, head(vb), jnp.where(lane == B_HDIM, 1.0, 0.0)).astype(BF16)

    @pl.when(t == pl.num_programs(1) - 1)
    def _():
        cend_ref[0] = carry_s[...]


def _fox_sel_matrix():
    sel = np.zeros((LANES, LANES), np.float32)
    for term in range(3):
        for hd in range(B_HEADS):
            sel[term * B_HEADS + hd, AUX_LANES * hd + term] = 1.0
            sel[term * B_HEADS + hd, AUX_LANES * hd + 3 + term] = -1.0
    return jnp.asarray(sel, BF16)


def _fox_proj(h, g_q, g_kv, w_q, w_k, w_v, w_f2, fg_b, c0, *, ts):
    bsz, t_len, d = h.shape
    assert t_len % ts == 0
    row = lambda v: v.reshape(1, d)
    sel = _fox_sel_matrix()
    head_shape = (bsz, B_HEADS, t_len, ATT_PAD)
    head_spec = pl.BlockSpec((1, B_HEADS, ts, ATT_PAD), lambda b, t: (b, 0, t, 0))
    return pl.pallas_call(
        functools.partial(_fox_proj_kernel, ts=ts),
        grid=(bsz, t_len // ts),
        in_specs=[
            pl.BlockSpec((1, ts, d), lambda b, t: (b, t, 0)),
            _const_spec((1, d)), _const_spec((1, d)),
            _const_spec(w_q.shape), _const_spec(w_k.shape), _const_spec(w_v.shape),
            _const_spec(w_f2.shape), _const_spec(fg_b.shape),
            _const_spec(sel.shape), _const_spec(c0.shape),
        ],
        out_specs=[head_spec, head_spec, head_spec,
                   pl.BlockSpec((1, 1, LANES), lambda b, t: (b, 0, 0))],
        out_shape=[jax.ShapeDtypeStruct(head_shape, BF16)] * 3 + [jax.ShapeDtypeStruct((bsz, 1, LANES), F32)],
        scratch_shapes=[pltpu.VMEM((1, LANES), F32)],
        compiler_params=_params(),
        name="fox_proj",
    )(h, row(g_q), row(g_kv), w_q, w_k, w_v, w_f2, fg_b, sel, c0)


def _fox_attn_kernel(*refs, tq, hp, n_meta):
    if n_meta:
        q_ref, k_ref, v_ref, km_ref, vm_ref, o_ref, kcat_s, vcat_s, s_a, s_b, p_a, p_b = refs
    else:
        q_ref, k_ref, v_ref, o_ref, kcat_s, vcat_s, s_a, s_b, p_a, p_b = refs
    t_len = q_ref.shape[2]
    n_keys = n_meta + t_len
    w_max = kcat_s.shape[1]
    for hh in range(hp):
        if n_meta:
            kcat_s[hh, 0:n_meta] = km_ref[0, hh]
            vcat_s[hh, 0:n_meta] = vm_ref[0, hh]
        kcat_s[hh, n_meta:n_keys] = k_ref[0, hh]
        vcat_s[hh, n_meta:n_keys] = v_ref[0, hh]
        if w_max > n_keys:
            vcat_s[hh, n_keys:w_max] = jnp.zeros((w_max - n_keys, vcat_s.shape[2]), BF16)

    units = [(i, hh) for i in range(-(-n_keys // tq)) for hh in range(hp)]

    def geometry(i):
        k0 = i * tq
        a, e = max(n_meta, k0), min(k0 + tq, n_keys)
        return k0, a, e, e - a, -(-e // LANES) * LANES

    def logit_pieces(u):
        (i, hh), s_s = units[u], (s_a, s_b)[u % 2]
        k0, a, e, nr, w = geometry(i)
        diag_ok = (lax.broadcasted_iota(jnp.int32, (nr, e - k0), 1)
                   <= lax.broadcasted_iota(jnp.int32, (nr, e - k0), 0) + (a - k0))

        def piece(kt):
            def run():
                q = q_ref[0, hh, a - n_meta:e - n_meta, :]
                if kt < k0:
                    s_s[0:nr, kt:kt + tq] = _dot_nt(q, kcat_s[hh, kt:kt + tq])
                else:
                    s_s[0:nr, k0:e] = jnp.where(diag_ok, _dot_nt(q, kcat_s[hh, k0:e]), MASKED)
                    if w > e:
                        s_s[0:nr, e:w] = jnp.full((nr, w - e), MASKED, F32)
            return run
        return [piece(kt) for kt in range(0, e, tq)]

    def softmax_pieces(u):
        (i, hh), s_s, p_s = units[u], (s_a, s_b)[u % 2], (p_a, p_b)[u % 2]
        k0, a, e, nr, w = geometry(i)
        rs = min(nr, ATT_STRIP)

        def piece(r):
            def run():
                x = s_s[r:r + rs, 0:w]
                m = jnp.max(x, axis=-1, keepdims=True)
                p_s[r:r + rs, 0:w] = jnp.exp2((x - m).astype(BF16))
            return run
        return [piece(r) for r in range(0, nr, rs)]

    def value_pieces(u):
        (i, hh), p_s = units[u], (p_a, p_b)[u % 2]
        k0, a, e, nr, w = geometry(i)

        def run():
            acc = jnp.dot(p_s[0:nr, 0:w], vcat_s[hh, 0:w], preferred_element_type=F32)
            o = acc[:, :B_HDIM] / acc[:, B_HDIM:B_HDIM + 1]
            o_ref[0, a - n_meta:e - n_meta, hh * B_HDIM:(hh + 1) * B_HDIM] = o.astype(BF16)
        return [run]

    for step in range(len(units) + 2):
        streams = [f(u) for f, u in ((value_pieces, step - 2), (softmax_pieces, step - 1), (logit_pieces, step))
                   if 0 <= u < len(units)]
        keyed = [((j + 0.5) / len(st), n, run) for n, st in enumerate(streams) for j, run in enumerate(st)]
        for _, _, run in sorted(keyed, key=lambda t: t[:2]):
            run()


def _fox_attn(q, k, v, meta_kv, *, tq, hp=HEADS_PER_STEP):
    bsz, heads, t_len, pad = q.shape
    assert heads % hp == 0
    head_spec = pl.BlockSpec((1, hp, t_len, pad), lambda b, g: (b, g, 0, 0))
    args, specs = [q, k, v], [head_spec] * 3
    n_meta = 0
    if meta_kv is not None:
        n_meta = meta_kv[0].shape[2]
        args += list(meta_kv)
        specs += [pl.BlockSpec((1, hp, n_meta, pad), lambda b, g: (0, g, 0, 0))] * 2
    w_max = -(-(n_meta + t_len) // LANES) * LANES
    return pl.pallas_call(
        functools.partial(_fox_attn_kernel, tq=tq, hp=hp, n_meta=n_meta),
        grid=(bsz, heads // hp),
        in_specs=specs,
        out_specs=pl.BlockSpec((1, t_len, hp * B_HDIM), lambda b, g: (b, 0, g)),
        out_shape=jax.ShapeDtypeStruct((bsz, t_len, heads * B_HDIM), BF16),
        scratch_shapes=[pltpu.VMEM((hp, w_max, pad), BF16)] * 2
                       + [pltpu.VMEM((tq, w_max), F32)] * 2 + [pltpu.VMEM((tq, w_max), BF16)] * 2,
        compiler_params=_params(),
        name="fox_attention",
    )(*args)


def _pad_cols(w, n):
    return jnp.pad(w, ((0, 0), (0, n - w.shape[1])))


def _prep_ffn(w_up, conv_w, w_down):
    up = jnp.concatenate([_pad_cols(w_up[:, :D_FF], D_FF_PAD), _pad_cols(w_up[:, D_FF:], D_FF_PAD)], axis=1)
    cw = jnp.concatenate([_pad_cols(conv_w[:, :D_FF], D_FF_PAD), _pad_cols(conv_w[:, D_FF:], D_FF_PAD)], axis=1)
    down = jnp.pad(w_down, ((0, D_FF_PAD - D_FF), (0, 0)))
    return up.astype(BF16), cw.astype(F32), down.astype(BF16)


def _trunk(h, params, carry_in, *, ts, ts_ffn, chunk, tq):
    s0, tail_a, tail_b, c0, meta_kv = carry_in
    p = params
    h1, s_fin = _hgrn2_layer(h, p["g"][0, 0], p["g"][0, 1], p["lb_logits"], p["head_gain"], p["w_in"],
                             p["a_w_out"], s0, ts=ts, chunk=chunk, layer=0)
    h2, tail_a_out = _ffn_layer(h1, p["g"][0, 2], p["g"][0, 3], *p["ffn"][0], tail_a, ts=ts_ffn)
    q, k, v, c_end = _fox_proj(h2, p["g"][1, 0], p["kv_norm"], p["w_q"], p["w_k"], p["w_v"],
                               p["w_f2"], p["fg_b"], c0, ts=ts)
    o = _fox_attn(q, k, v, meta_kv, tq=tq)
    h4, tail_b_out = _ffn_layer(h2, p["g"][1, 2], p["g"][1, 3], *p["ffn"][1], tail_b, ts=ts_ffn,
                                attn=(o, p["b_w_out"], p["g"][1, 1]))
    return h4, (s_fin[0], tail_a_out[0], tail_b_out[0], c_end[0], (k[:1], v[:1]))


def kernel(x, meta_tokens, norm_gains, a_w_in, a_lb_logits, a_head_norm, a_w_out, kv_norm, kv_w, fg_b,
           b_w_q, b_w_out, ffn_w_up, ffn_conv, ffn_w_down):
    d = D_MODEL
    assert x.shape[2] == d and x.shape[1] % SEQ_TILE == 0 and meta_tokens.shape == (N_META, d)
    assert norm_gains.shape[0] == 2 and a_w_in.shape[0] == 1 and b_w_q.shape[0] == 1

    w_f = jnp.pad(kv_w[:, 2 * d:], ((0, 0), (0, LANES - B_HEADS))).astype(F32)
    w_f_hi = w_f.astype(BF16)
    params = {
        "g": norm_gains.astype(F32),
        "lb_logits": a_lb_logits.astype(F32),
        "head_gain": a_head_norm[0].astype(F32),
        "w_in": a_w_in[0].astype(BF16),
        "a_w_out": a_w_out[0].astype(BF16),
        "kv_norm": kv_norm.astype(F32),
        "w_q": b_w_q[0].astype(BF16),
        "w_k": kv_w[:, :d].astype(BF16),
        "w_v": kv_w[:, d:2 * d].astype(BF16),
        "w_f2": jnp.concatenate([w_f_hi, (w_f - w_f_hi.astype(F32)).astype(BF16)], axis=1),
        "fg_b": jnp.pad(fg_b.astype(F32), (0, LANES - B_HEADS)).reshape(1, LANES),
        "b_w_out": b_w_out[0].astype(BF16),
        "ffn": [_prep_ffn(ffn_w_up[l], ffn_conv[l], ffn_w_down[l]) for l in range(2)],
    }

    zero_carry = (jnp.zeros((A_HEADS, A_DK, A_DK), F32),
                  jnp.zeros((SUBLANES, 2 * D_FF_PAD), F32), jnp.zeros((SUBLANES, 2 * D_FF_PAD), F32),
                  jnp.zeros((1, LANES), F32), None)
    _, meta_carry = _trunk(meta_tokens[None].astype(F32), params, zero_carry,
                           ts=N_META, ts_ffn=N_META, chunk=N_META, tq=N_META)
    out, _ = _trunk(x, params, meta_carry, ts=SEQ_TILE, ts_ffn=FFN_TILE, chunk=A_CHUNK, tq=Q_TILE)
    return out
```

```python
import functools

import numpy as np
import jax
import jax.numpy as jnp
from jax import lax
from jax.experimental import pallas as pl
from jax.experimental.pallas import tpu as pltpu

F32 = jnp.float32
BF16 = jnp.bfloat16

D_MODEL = 1024
N_META = 16
A_HEADS = 8
A_DK = D_MODEL // A_HEADS
A_CHUNK = 64
B_HEADS = 16
B_HDIM = D_MODEL // B_HEADS
D_FF = 2752
EPS = 1e-6

LANES = 128
SUBLANES = 8
MXU_DIM = 256
D_FF_PAD = -(-D_FF // MXU_DIM) * MXU_DIM
ATT_PAD = LANES
AUX_LANES = LANES // B_HEADS
VMEM_LIMIT = 56 * 1024 * 1024

SEQ_TILE = 512
GLA_TILE = 256
FFN_TILE = 512
Q_TILE = 256
PROJ_COLS = MXU_DIM
GATE_COLS = MXU_DIM
CONV_ROWS = 64
CONV_COLS = 256
HEADS_PER_STEP = 4
ATT_STRIP = 16
MASKED = -1e30
LOG2E = 1.4426950408889634


def _rms(x, g):
    return x * lax.rsqrt(jnp.mean(x * x, axis=-1, keepdims=True) + EPS) * g


def _sigmoid(x):
    return 1.0 / (1.0 + jnp.exp(-x))


def _split3(x):
    hi = x.astype(BF16)
    r = x - hi.astype(F32)
    mid = r.astype(BF16)
    lo = (r - mid.astype(F32)).astype(BF16)
    return hi, mid, lo


def _tril(n):
    row = lax.broadcasted_iota(jnp.int32, (n, n), 0)
    col = lax.broadcasted_iota(jnp.int32, (n, n), 1)
    return row >= col


def _cumsum_rows(x, tril_bf16):
    hi, mid, lo = _split3(x)
    dot = functools.partial(jnp.dot, preferred_element_type=F32)
    return dot(tril_bf16, hi) + dot(tril_bf16, mid) + dot(tril_bf16, lo)


def _dot_nt(a, b):
    return lax.dot_general(a, b, (((1,), (1,)), ((), ())), preferred_element_type=F32)


def _const_spec(shape):
    nd = len(shape)
    return pl.BlockSpec(shape, lambda *_: (0,) * nd, pipeline_mode=pl.Buffered(1))


def _params():
    return pltpu.CompilerParams(dimension_semantics=("arbitrary", "arbitrary"),
                                vmem_limit_bytes=VMEM_LIMIT)


def _hgrn2_kernel(x_ref, gpre_ref, gpost_ref, lbl_ref, hg_ref, win_ref, wout_ref, s0_ref,
                  out_ref, sfin_ref, proj_s, qin_s, kin_s, kout_s, v_s, dec_s, y_s, st_s, *, ts, chunk, layer):
    t = pl.program_id(1)
    n_chunks = ts // chunk

    @pl.when(t == 0)
    def _():
        st_s[...] = s0_ref[...]

    lbl = lbl_ref[...]
    e = jnp.exp(lbl - jnp.max(lbl, axis=0, keepdims=True))
    lb = jnp.sum(e[:layer + 1], axis=0, keepdims=True) / jnp.sum(e, axis=0, keepdims=True)

    xn = _rms(x_ref[0], gpre_ref[...]).astype(BF16)
    proj_s[...] = jnp.dot(xn, win_ref[...], preferred_element_type=F32)

    tril3 = jnp.concatenate([_tril(chunk).astype(BF16)] * 3, axis=1)
    for c in range(n_chunks):
        rows = slice(c * chunk, (c + 1) * chunk)
        for c0 in range(0, D_MODEL, GATE_COLS):
            cols = slice(c0, c0 + GATE_COLS)
            lbc = lb[:, cols]
            f = lbc + (1.0 - lbc) * _sigmoid(proj_s[rows, D_MODEL + c0:D_MODEL + c0 + GATE_COLS])
            k = 1.0 - f
            b = jnp.dot(tril3, jnp.concatenate(_split3(jnp.log(f)), axis=0), preferred_element_type=F32)
            dec = jnp.exp(b[chunk - 1:chunk, :])
            k_in = k * jnp.exp(-b)
            qin_s[rows, cols] = (proj_s[rows, cols] * jnp.exp(b)).astype(BF16)
            kin_s[rows, cols] = k_in.astype(BF16)
            kout_s[rows, cols] = (k_in * dec).astype(BF16)
            v_s[rows, cols] = proj_s[rows, 2 * D_MODEL + c0:2 * D_MODEL + c0 + GATE_COLS].astype(BF16)
            dec_s[c:c + 1, cols] = dec

    gt = min(ts, GLA_TILE)
    sub_chunks = gt // chunk
    row_c = lax.broadcasted_iota(jnp.int32, (gt, gt), 0)
    col_c = lax.broadcasted_iota(jnp.int32, (gt, gt), 1)
    intra = (row_c >= col_c) & (row_c // chunk == col_c // chunk)
    vt_chunk = lax.broadcasted_iota(jnp.int32, (A_DK, gt), 1) // chunk
    hgain = hg_ref[...]
    for r0 in range(0, ts, gt):
        rows = slice(r0, r0 + gt)
        for h in range(A_HEADS):
            cols = slice(h * A_DK, (h + 1) * A_DK)
            q_in = qin_s[rows, cols]
            attn = jnp.where(intra, _dot_nt(q_in, kin_s[rows, cols]), 0.0).astype(BF16)
            o_intra = jnp.dot(attn, v_s[rows, cols], preferred_element_type=F32)
            v_t = proj_s[rows, 2 * D_MODEL + h * A_DK:2 * D_MODEL + (h + 1) * A_DK].T.astype(BF16)
            v_t_blocks = jnp.concatenate([jnp.where(vt_chunk == c, v_t, jnp.zeros_like(v_t))
                                          for c in range(sub_chunks)], axis=0)
            d_st = jnp.dot(v_t_blocks, kout_s[rows, cols], preferred_element_type=F32)
            st = st_s[h]
            o_inter = []
            for c in range(sub_chunks):
                o_inter.append(_dot_nt(q_in[c * chunk:(c + 1) * chunk], st.astype(BF16)))
                dec = dec_s[r0 // chunk + c:r0 // chunk + c + 1, cols]
                st = st * dec + d_st[c * A_DK:(c + 1) * A_DK]
            st_s[h] = st
            o = o_intra + jnp.concatenate(o_inter, axis=0)
            o = o * lax.rsqrt(jnp.mean(o * o, axis=-1, keepdims=True) + EPS)
            gg = proj_s[rows, 3 * D_MODEL + h * A_DK:3 * D_MODEL + (h + 1) * A_DK]
            y_s[rows, cols] = (o * hgain[:, cols] * (gg * _sigmoid(gg))).astype(BF16)

    mix = jnp.dot(y_s[...], wout_ref[...], preferred_element_type=F32)
    out_ref[0] = x_ref[0] + _rms(mix, gpost_ref[...])

    @pl.when(t == pl.num_programs(1) - 1)
    def _():
        sfin_ref[0] = st_s[...]


def _hgrn2_layer(h, g_pre, g_post, lb_logits, head_gain, w_in, w_out, s0, *, ts, chunk, layer):
    bsz, t_len, d = h.shape
    assert t_len % ts == 0 and ts % chunk == 0
    row = lambda v: v.reshape(1, d)
    st_shape = (A_HEADS, A_DK, A_DK)
    out, s_fin = pl.pallas_call(
        functools.partial(_hgrn2_kernel, ts=ts, chunk=chunk, layer=layer),
        grid=(bsz, t_len // ts),
        in_specs=[
            pl.BlockSpec((1, ts, d), lambda b, t: (b, t, 0)),
            _const_spec((1, d)), _const_spec((1, d)), _const_spec(lb_logits.shape), _const_spec((1, d)),
            _const_spec(w_in.shape), _const_spec(w_out.shape), _const_spec(st_shape),
        ],
        out_specs=[
            pl.BlockSpec((1, ts, d), lambda b, t: (b, t, 0)),
            pl.BlockSpec((1,) + st_shape, lambda b, t: (b, 0, 0, 0)),
        ],
        out_shape=[jax.ShapeDtypeStruct(h.shape, F32),
                   jax.ShapeDtypeStruct((bsz,) + st_shape, F32)],
        scratch_shapes=[pltpu.VMEM((ts, 4 * d), F32)] + [pltpu.VMEM((ts, d), BF16)] * 4
                       + [pltpu.VMEM((max(ts // chunk, SUBLANES), d), F32), pltpu.VMEM((ts, d), BF16),
                          pltpu.VMEM(st_shape, F32)],
        compiler_params=_params(),
        name="hgrn2_mixer",
    )(h, row(g_pre), row(g_post), lb_logits, row(head_gain), w_in, w_out, s0)
    return out, s_fin


def _ffn_kernel(*refs, ts, with_attn):
    if with_attn:
        (h_ref, o_ref, wo_ref, gmix_ref, g2_ref, g3_ref, wup_ref, conv_ref, wdown_ref, tail0_ref,
         out_ref, tail_ref, u_s, act_s) = refs
    else:
        (h_ref, g2_ref, g3_ref, wup_ref, conv_ref, wdown_ref, tail0_ref,
         out_ref, tail_ref, u_s, act_s) = refs
    t = pl.program_id(1)
    halo = SUBLANES

    @pl.when(t == 0)
    def _():
        u_s[0:halo, :] = tail0_ref[...]

    h = h_ref[0]
    if with_attn:
        mix = jnp.dot(o_ref[0], wo_ref[...], preferred_element_type=F32)
        h = h + _rms(mix, gmix_ref[...])
    xn = _rms(h, g2_ref[...]).astype(BF16)
    rb = min(CONV_ROWS, ts)
    for j in range(D_FF_PAD // CONV_COLS):
        starts = (j * CONV_COLS, D_FF_PAD + j * CONV_COLS)
        for c0 in starts:
            u_s[halo:halo + ts, c0:c0 + CONV_COLS] = jnp.dot(
                xn, wup_ref[:, c0:c0 + CONV_COLS], preferred_element_type=F32)
        for r0 in range(0, ts, rb):
            def conv(c0):
                w = conv_ref[:, c0:c0 + CONV_COLS]
                blk = u_s[r0:r0 + rb + halo, c0:c0 + CONV_COLS]
                tap = lambda d: (pltpu.roll(blk, d, 0) if d else blk)[halo:halo + rb]
                return w[0:1] * tap(2) + w[1:2] * tap(1) + w[2:3] * tap(0)
            cg, cv = conv(starts[0]), conv(starts[1])
            act_s[r0:r0 + rb, j * CONV_COLS:(j + 1) * CONV_COLS] = (cg * jax.nn.sigmoid(cg) * cv).astype(BF16)
    u_s[0:halo, :] = u_s[ts:ts + halo, :]

    ff = jnp.dot(act_s[...], wdown_ref[...], preferred_element_type=F32)
    out_ref[0] = h + _rms(ff, g3_ref[...])

    @pl.when(t == pl.num_programs(1) - 1)
    def _():
        tail_ref[0] = u_s[0:halo, :]


def _ffn_layer(h, g2, g3, w_up, conv_w, w_down, tail0, *, ts, attn=None):
    bsz, t_len, d = h.shape
    assert t_len % ts == 0
    row = lambda v: v.reshape(1, d)
    tile = pl.BlockSpec((1, ts, d), lambda b, t: (b, t, 0))
    args, specs = [h], [tile]
    if attn is not None:
        o, w_o, g_mix = attn
        args += [o, w_o, row(g_mix)]
        specs += [tile, _const_spec(w_o.shape), _const_spec((1, d))]
    args += [row(g2), row(g3), w_up, conv_w, w_down, tail0]
    specs += [_const_spec((1, d)), _const_spec((1, d)), _const_spec(w_up.shape), _const_spec(conv_w.shape),
              _const_spec(w_down.shape), _const_spec(tail0.shape)]
    out, tail = pl.pallas_call(
        functools.partial(_ffn_kernel, ts=ts, with_attn=attn is not None),
        grid=(bsz, t_len // ts),
        in_specs=specs,
        out_specs=[tile, pl.BlockSpec((1, SUBLANES, 2 * D_FF_PAD), lambda b, t: (b, 0, 0))],
        out_shape=[jax.ShapeDtypeStruct(h.shape, F32),
                   jax.ShapeDtypeStruct((bsz, SUBLANES, 2 * D_FF_PAD), F32)],
        scratch_shapes=[pltpu.VMEM((ts + SUBLANES, 2 * D_FF_PAD), F32), pltpu.VMEM((ts, D_FF_PAD), BF16)],
        compiler_params=_params(),
        name="conv_ffn",
    )(*args)
    return out, tail


def _fox_proj_kernel(h_ref, gq_ref, gkv_ref, wq_ref, wk_ref, wv_ref, wf2_ref, fgb_ref, sel_ref,
                     c0_ref, q_out, k_out, v_out, cend_ref, carry_s, *, ts):
    t = pl.program_id(1)

    @pl.when(t == 0)
    def _():
        carry_s[...] = c0_ref[...]

    h = h_ref[0]
    y = h * lax.rsqrt(jnp.mean(h * h, axis=-1, keepdims=True) + EPS)
    xq = (y * gq_ref[...]).astype(BF16)
    xkv = y * gkv_ref[...]
    xkv_hi = xkv.astype(BF16)
    xkv_lo = (xkv - xkv_hi.astype(F32)).astype(BF16)
    dot = functools.partial(jnp.dot, preferred_element_type=F32)
    zf2 = dot(xkv_hi, wf2_ref[...])
    zf = (zf2[:, :LANES] + zf2[:, LANES:] + dot(xkv_lo, wf2_ref[:, :LANES])
          + fgb_ref[...])
    ls = jnp.minimum(zf, 0.0) - jnp.log(1.0 + jnp.exp(-jnp.abs(zf)))
    c = carry_s[...] + _cumsum_rows(ls, _tril(ts).astype(BF16))
    carry_s[...] = c[ts - 1:ts, :]

    c2 = c * np.float32(LOG2E)
    c_hi = c2.astype(BF16).astype(F32)
    r = c2 - c_hi
    c_mid = r.astype(BF16).astype(F32)
    c_lo = (r - c_mid).astype(BF16).astype(F32)
    lane = lax.broadcasted_iota(jnp.int32, (ts, LANES), 1)
    parts = jnp.where(lane < B_HEADS, c_hi,
                      jnp.where(lane < 2 * B_HEADS, pltpu.roll(c_mid, B_HEADS, 1),
                                jnp.where(lane < 3 * B_HEADS, pltpu.roll(c_lo, 2 * B_HEADS, 1), 0.0)))
    aux = dot(parts.astype(BF16), sel_ref[...])
    data = lane < B_HDIM
    plus = lane < B_HDIM + 3
    minus = lane < B_HDIM + 6
    heads_per_block = PROJ_COLS // B_HDIM
    for c0 in range(0, D_MODEL, PROJ_COLS):
        cols = slice(c0, c0 + PROJ_COLS)
        qb = dot(xq, wq_ref[:, cols]) * np.float32(LOG2E / np.sqrt(B_HDIM))
        kb = dot(xkv_hi, wk_ref[:, cols])
        vb = dot(xkv_hi, wv_ref[:, cols])
        hd0 = c0 // B_HDIM
        for j in range(heads_per_block):
            def head(x):
                x = x[:, (j // 2) * LANES:(j // 2 + 1) * LANES]
                return pltpu.roll(x, B_HDIM, 1) if j % 2 else x
            hd = hd0 + j
            shift = (B_HDIM - AUX_LANES * hd) % LANES
            a = pltpu.roll(aux, shift, 1) if shift else aux
            q_out[0, hd] = jnp.where(data, head(qb), jnp.where(plus, a, jnp.where(minus, 1.0, 0.0))).astype(BF16)
            k_out[0, hd] = jnp.where(data, head(kb), jnp.where(plus, 1.0, jnp.where(minus, a, 0.0))).astype(BF16)
            v_out[0, hd] = jnp.where(data, head(vb), jnp.where(lane == B_HDIM, 1.0, 0.0)).astype(BF16)

    @pl.when(t == pl.num_programs(1) - 1)
    def _():
        cend_ref[0] = carry_s[...]


def _fox_sel_matrix():
    sel = np.zeros((LANES, LANES), np.float32)
    for term in range(3):
        for hd in range(B_HEADS):
            sel[term * B_HEADS + hd, AUX_LANES * hd + term] = 1.0
            sel[term * B_HEADS + hd, AUX_LANES * hd + 3 + term] = -1.0
    return jnp.asarray(sel, BF16)


def _fox_proj(h, g_q, g_kv, w_q, w_k, w_v, w_f2, fg_b, c0, *, ts):
    bsz, t_len, d = h.shape
    assert t_len % ts == 0
    row = lambda v: v.reshape(1, d)
    sel = _fox_sel_matrix()
    head_shape = (bsz, B_HEADS, t_len, ATT_PAD)
    head_spec = pl.BlockSpec((1, B_HEADS, ts, ATT_PAD), lambda b, t: (b, 0, t, 0))
    return pl.pallas_call(
        functools.partial(_fox_proj_kernel, ts=ts),
        grid=(bsz, t_len // ts),
        in_specs=[
            pl.BlockSpec((1, ts, d), lambda b, t: (b, t, 0)),
            _const_spec((1, d)), _const_spec((1, d)),
            _const_spec(w_q.shape), _const_spec(w_k.shape), _const_spec(w_v.shape),
            _const_spec(w_f2.shape), _const_spec(fg_b.shape),
            _const_spec(sel.shape), _const_spec(c0.shape),
        ],
        out_specs=[head_spec, head_spec, head_spec,
                   pl.BlockSpec((1, 1, LANES), lambda b, t: (b, 0, 0))],
        out_shape=[jax.ShapeDtypeStruct(head_shape, BF16)] * 3 + [jax.ShapeDtypeStruct((bsz, 1, LANES), F32)],
        scratch_shapes=[pltpu.VMEM((1, LANES), F32)],
        compiler_params=_params(),
        name="fox_proj",
    )(h, row(g_q), row(g_kv), w_q, w_k, w_v, w_f2, fg_b, sel, c0)


def _fox_attn_kernel(*refs, tq, hp, n_meta):
    if n_meta:
        q_ref, k_ref, v_ref, km_ref, vm_ref, o_ref, kcat_s, vcat_s, s_a, s_b, p_a, p_b = refs
    else:
        q_ref, k_ref, v_ref, o_ref, kcat_s, vcat_s, s_a, s_b, p_a, p_b = refs
    t_len = q_ref.shape[2]
    n_keys = n_meta + t_len
    w_max = kcat_s.shape[1]
    for hh in range(hp):
        if n_meta:
            kcat_s[hh, 0:n_meta] = km_ref[0, hh]
            vcat_s[hh, 0:n_meta] = vm_ref[0, hh]
        kcat_s[hh, n_meta:n_keys] = k_ref[0, hh]
        vcat_s[hh, n_meta:n_keys] = v_ref[0, hh]
        if w_max > n_keys:
            vcat_s[hh, n_keys:w_max] = jnp.zeros((w_max - n_keys, vcat_s.shape[2]), BF16)

    units = [(i, hh) for i in range(-(-n_keys // tq)) for hh in range(hp)]

    def geometry(i):
        k0 = i * tq
        a, e = max(n_meta, k0), min(k0 + tq, n_keys)
        return k0, a, e, e - a, -(-e // LANES) * LANES

    def logit_pieces(u):
        (i, hh), s_s = units[u], (s_a, s_b)[u % 2]
        k0, a, e, nr, w = geometry(i)
        diag_ok = (lax.broadcasted_iota(jnp.int32, (nr, e - k0), 1)
                   <= lax.broadcasted_iota(jnp.int32, (nr, e - k0), 0) + (a - k0))

        def piece(kt):
            def run():
                q = q_ref[0, hh, a - n_meta:e - n_meta, :]
                if kt < k0:
                    s_s[0:nr, kt:kt + tq] = _dot_nt(q, kcat_s[hh, kt:kt + tq])
                else:
                    s_s[0:nr, k0:e] = jnp.where(diag_ok, _dot_nt(q, kcat_s[hh, k0:e]), MASKED)
                    if w > e:
                        s_s[0:nr, e:w] = jnp.full((nr, w - e), MASKED, F32)
            return run
        return [piece(kt) for kt in range(0, e, tq)]

    def softmax_pieces(u):
        (i, hh), s_s, p_s = units[u], (s_a, s_b)[u % 2], (p_a, p_b)[u % 2]
        k0, a, e, nr, w = geometry(i)
        rs = min(nr, ATT_STRIP)

        def piece(r):
            def run():
                x = s_s[r:r + rs, 0:w]
                m = jnp.max(x, axis=-1, keepdims=True)
                p_s[r:r + rs, 0:w] = jnp.exp2((x - m).astype(BF16))
            return run
        return [piece(r) for r in range(0, nr, rs)]

    def value_pieces(u):
        (i, hh), p_s = units[u], (p_a, p_b)[u % 2]
        k0, a, e, nr, w = geometry(i)

        def run():
            acc = jnp.dot(p_s[0:nr, 0:w], vcat_s[hh, 0:w], preferred_element_type=F32)
            o = acc[:, :B_HDIM] / acc[:, B_HDIM:B_HDIM + 1]
            o_ref[0, a - n_meta:e - n_meta, hh * B_HDIM:(hh + 1) * B_HDIM] = o.astype(BF16)
        return [run]

    for step in range(len(units) + 2):
        streams = [f(u) for f, u in ((value_pieces, step - 2), (softmax_pieces, step - 1), (logit_pieces, step))
                   if 0 <= u < len(units)]
        keyed = [((j + 0.5) / len(st), n, run) for n, st in enumerate(streams) for j, run in enumerate(st)]
        for _, _, run in sorted(keyed, key=lambda t: t[:2]):
            run()


def _fox_attn(q, k, v, meta_kv, *, tq, hp=HEADS_PER_STEP):
    bsz, heads, t_len, pad = q.shape
    assert heads % hp == 0
    head_spec = pl.BlockSpec((1, hp, t_len, pad), lambda b, g: (b, g, 0, 0))
    args, specs = [q, k, v], [head_spec] * 3
    n_meta = 0
    if meta_kv is not None:
        n_meta = meta_kv[0].shape[2]
        args += list(meta_kv)
        specs += [pl.BlockSpec((1, hp, n_meta, pad), lambda b, g: (0, g, 0, 0))] * 2
    w_max = -(-(n_meta + t_len) // LANES) * LANES
    return pl.pallas_call(
        functools.partial(_fox_attn_kernel, tq=tq, hp=hp, n_meta=n_meta),
        grid=(bsz, heads // hp),
        in_specs=specs,
        out_specs=pl.BlockSpec((1, t_len, hp * B_HDIM), lambda b, g: (b, 0, g)),
        out_shape=jax.ShapeDtypeStruct((bsz, t_len, heads * B_HDIM), BF16),
        scratch_shapes=[pltpu.VMEM((hp, w_max, pad), BF16)] * 2
                       + [pltpu.VMEM((tq, w_max), F32)] * 2 + [pltpu.VMEM((tq, w_max), BF16)] * 2,
        compiler_params=_params(),
        name="fox_attention",
    )(*args)


def _pad_cols(w, n):
    return jnp.pad(w, ((0, 0), (0, n - w.shape[1])))


def _prep_ffn(w_up, conv_w, w_down):
    up = jnp.concatenate([_pad_cols(w_up[:, :D_FF], D_FF_PAD), _pad_cols(w_up[:, D_FF:], D_FF_PAD)], axis=1)
    cw = jnp.concatenate([_pad_cols(conv_w[:, :D_FF], D_FF_PAD), _pad_cols(conv_w[:, D_FF:], D_FF_PAD)], axis=1)
    down = jnp.pad(w_down, ((0, D_FF_PAD - D_FF), (0, 0)))
    return up.astype(BF16), cw.astype(F32), down.astype(BF16)


def _trunk(h, params, carry_in, *, ts, ts_ffn, chunk, tq):
    s0, tail_a, tail_b, c0, meta_kv = carry_in
    p = params
    h1, s_fin = _hgrn2_layer(h, p["g"][0, 0], p["g"][0, 1], p["lb_logits"], p["head_gain"], p["w_in"],
                             p["a_w_out"], s0, ts=ts, chunk=chunk, layer=0)
    h2, tail_a_out = _ffn_layer(h1, p["g"][0, 2], p["g"][0, 3], *p["ffn"][0], tail_a, ts=ts_ffn)
    q, k, v, c_end = _fox_proj(h2, p["g"][1, 0], p["kv_norm"], p["w_q"], p["w_k"], p["w_v"],
                               p["w_f2"], p["fg_b"], c0, ts=ts)
    o = _fox_attn(q, k, v, meta_kv, tq=tq)
    h4, tail_b_out = _ffn_layer(h2, p["g"][1, 2], p["g"][1, 3], *p["ffn"][1], tail_b, ts=ts_ffn,
                                attn=(o, p["b_w_out"], p["g"][1, 1]))
    return h4, (s_fin[0], tail_a_out[0], tail_b_out[0], c_end[0], (k[:1], v[:1]))


def kernel(x, meta_tokens, norm_gains, a_w_in, a_lb_logits, a_head_norm, a_w_out, kv_norm, kv_w, fg_b,
           b_w_q, b_w_out, ffn_w_up, ffn_conv, ffn_w_down):
    d = D_MODEL
    assert x.shape[2] == d and x.shape[1] % SEQ_TILE == 0 and meta_tokens.shape == (N_META, d)
    assert norm_gains.shape[0] == 2 and a_w_in.shape[0] == 1 and b_w_q.shape[0] == 1

    w_f = jnp.pad(kv_w[:, 2 * d:], ((0, 0), (0, LANES - B_HEADS))).astype(F32)
    w_f_hi = w_f.astype(BF16)
    params = {
        "g": norm_gains.astype(F32),
        "lb_logits": a_lb_logits.astype(F32),
        "head_gain": a_head_norm[0].astype(F32),
        "w_in": a_w_in[0].astype(BF16),
        "a_w_out": a_w_out[0].astype(BF16),
        "kv_norm": kv_norm.astype(F32),
        "w_q": b_w_q[0].astype(BF16),
        "w_k": kv_w[:, :d].astype(BF16),
        "w_v": kv_w[:, d:2 * d].astype(BF16),
        "w_f2": jnp.concatenate([w_f_hi, (w_f - w_f_hi.astype(F32)).astype(BF16)], axis=1),
        "fg_b": jnp.pad(fg_b.astype(F32), (0, LANES - B_HEADS)).reshape(1, LANES),
        "b_w_out": b_w_out[0].astype(BF16),
        "ffn": [_prep_ffn(ffn_w_up[l], ffn_conv[l], ffn_w_down[l]) for l in range(2)],
    }

    zero_carry = (jnp.zeros((A_HEADS, A_DK, A_DK), F32),
                  jnp.zeros((SUBLANES, 2 * D_FF_PAD), F32), jnp.zeros((SUBLANES, 2 * D_FF_PAD), F32),
                  jnp.zeros((1, LANES), F32), None)
    _, meta_carry = _trunk(meta_tokens[None].astype(F32), params, zero_carry,
                           ts=N_META, ts_ffn=N_META, chunk=N_META, tq=N_META)
    out, _ = _trunk(x, params, meta_carry, ts=SEQ_TILE, ts_ffn=FFN_TILE, chunk=A_CHUNK, tq=Q_TILE)
    return out
```
